```python
import jax, jax.numpy as jnp
from jax import lax
import numpy as np

D_MODEL = 1024
BATCH = 8
SEQ = 2048
DEPTH = 2
DEC_BATCH = 128
DEC_SEQ = 1
PAST_LEN = 16384
PAGE_SIZE = 128

MIX_WIDTH = D_MODEL
GLA_WIDTH = MIX_WIDTH // 2
RET_WIDTH = MIX_WIDTH - GLA_WIDTH
GLA_HEADS = 4
GLA_DV = GLA_WIDTH // GLA_HEADS
GLA_DK = GLA_DV // 2
GLA_RANK = 16
GLA_GATE_TAU = 16.0
RET_HEADS = 4
RET_DV = RET_WIDTH // RET_HEADS
RET_DK = RET_DV // 2
IN_COLS = (2 * GLA_HEADS * GLA_DK + 2 * GLA_WIDTH + GLA_RANK
           + 2 * RET_HEADS * RET_DK + 2 * RET_WIDTH)
CHUNK = 64
ROPE_BASE = 10000.0
EPS = 1e-6
GN_EPS = 1e-5

kernel_name = 'hymba_gla_retnet_hybrid_step'

F32 = jnp.float32


def _rmsnorm(x, w):
    xf = x.astype(F32)
    y = xf * lax.rsqrt(jnp.mean(xf * xf, axis=-1, keepdims=True) + EPS) * w.astype(F32)
    return y.astype(x.dtype)


def _split_cols(z):
    sizes = (GLA_HEADS * GLA_DK, GLA_HEADS * GLA_DK, GLA_WIDTH, GLA_WIDTH, GLA_RANK,
             RET_HEADS * RET_DK, RET_HEADS * RET_DK, RET_WIDTH, RET_WIDTH)
    offs = np.cumsum(sizes)[:-1].tolist()
    return jnp.split(z, offs, axis=-1)


def _rope(x, pos):
    half = x.shape[-1] // 2
    inv = ROPE_BASE ** (-jnp.arange(half, dtype=F32) / half)
    ang = pos.astype(F32)[:, None] * inv[None, :]
    cos = jnp.cos(ang)[:, None, :]
    sin = jnp.sin(ang)[:, None, :]
    x1, x2 = x[..., :half], x[..., half:]
    return jnp.concatenate([x1 * cos - x2 * sin, x1 * sin + x2 * cos], axis=-1)


def _ret_log_gamma():
    return jnp.log(1.0 - jnp.power(2.0, -5.0 - jnp.arange(RET_HEADS, dtype=F32)))


def _to_chunks(a):
    b, t, h, d = a.shape
    return a.reshape(b, t // CHUNK, CHUNK, h, d).transpose(1, 0, 3, 2, 4)


def _from_chunks(o):
    n, b, h, c, d = o.shape
    return o.transpose(1, 0, 3, 2, 4).reshape(b, n * c, h, d)


def _gla_chunked(q, k, v, la, s0):
    mask = jnp.tril(jnp.ones((CHUNK, CHUNK), dtype=bool))

    def step(s, xs):
        qc, kc, vc, lc = xs
        b = jnp.cumsum(lc, axis=2)
        qd = qc * jnp.exp(b)
        kd = kc * jnp.exp(-b)
        att = jnp.where(mask, jnp.einsum('bhtk,bhsk->bhts', qd, kd), 0.0)
        o = jnp.einsum('bhts,bhsv->bhtv', att, vc) + jnp.einsum('bhtk,bhkv->bhtv', qd, s)
        b_last = b[:, :, -1:, :]
        s_new = (jnp.exp(b_last[:, :, 0, :])[..., None] * s
                 + jnp.einsum('bhsk,bhsv->bhkv', kc * jnp.exp(b_last - b), vc))
        return s_new, o

    s, o = lax.scan(step, s0, (_to_chunks(q), _to_chunks(k), _to_chunks(v), _to_chunks(la)))
    return _from_chunks(o), s


def _gla_recurrent(q, k, v, la, s0):
    def step(s, xs):
        qt, kt, vt, lt = xs
        s = jnp.exp(lt)[..., None] * s + kt[..., None] * vt[..., None, :]
        return s, jnp.einsum('bhk,bhkv->bhv', qt, s)

    s, o = lax.scan(step, s0, (q.swapaxes(0, 1), k.swapaxes(0, 1), v.swapaxes(0, 1), la.swapaxes(0, 1)))
    return o.swapaxes(0, 1), s


def _ret_chunked(q, k, v, s0):
    lg = _ret_log_gamma()
    idx = jnp.arange(CHUNK, dtype=F32)
    rel = idx[:, None] - idx[None, :]
    dmat = jnp.where(rel >= 0, jnp.exp(lg[:, None, None] * jnp.maximum(rel, 0.0)), 0.0)
    q_dec = jnp.exp(lg[:, None] * (idx + 1.0)[None, :])[None, :, :, None]
    k_dec = jnp.exp(lg[:, None] * (CHUNK - 1.0 - idx)[None, :])[None, :, :, None]
    chunk_dec = jnp.exp(lg * CHUNK)[None, :, None, None]

    def step(s, xs):
        qc, kc, vc = xs
        att = jnp.einsum('bhtk,bhsk->bhts', qc, kc) * dmat[None]
        o = jnp.einsum('bhts,bhsv->bhtv', att, vc) + jnp.einsum('bhtk,bhkv->bhtv', qc * q_dec, s)
        s_new = chunk_dec * s + jnp.einsum('bhsk,bhsv->bhkv', kc * k_dec, vc)
        return s_new, o

    s, o = lax.scan(step, s0, (_to_chunks(q), _to_chunks(k), _to_chunks(v)))
    return _from_chunks(o), s


def _ret_recurrent(q, k, v, s0):
    gamma = jnp.exp(_ret_log_gamma())[None, :, None, None]

    def step(s, xs):
        qt, kt, vt = xs
        s = gamma * s + kt[..., None] * vt[..., None, :]
        return s, jnp.einsum('bhk,bhkv->bhv', qt, s)

    s, o = lax.scan(step, s0, (q.swapaxes(0, 1), k.swapaxes(0, 1), v.swapaxes(0, 1)))
    return o.swapaxes(0, 1), s


def _layer(x, pos, s_gla, s_ret, norm_w, w_in, gla_w2, gla_b, gla_norm_w,
           ret_norm_w, ret_norm_b, w_out, chunked):
    B, T, _ = x.shape
    h = _rmsnorm(x, norm_w)
    z = jnp.einsum('btd,dc->btc', h, w_in).astype(F32)
    gq, gk, gv, gg, glr, rq, rk, rv, rg = _split_cols(z)
    q_a = gq.reshape(B, T, GLA_HEADS, GLA_DK) * (GLA_DK ** -0.5)
    k_a = gk.reshape(B, T, GLA_HEADS, GLA_DK)
    v_a = gv.reshape(B, T, GLA_HEADS, GLA_DV)
    la = (jax.nn.log_sigmoid(glr @ gla_w2.astype(F32) + gla_b.astype(F32)) / GLA_GATE_TAU
          ).reshape(B, T, GLA_HEADS, GLA_DK)
    q_b = _rope(rq.reshape(B, T, RET_HEADS, RET_DK), pos)
    k_b = _rope(rk.reshape(B, T, RET_HEADS, RET_DK), pos) * (RET_DK ** -0.5)
    v_b = rv.reshape(B, T, RET_HEADS, RET_DV)
    if chunked:
        o_a, s_gla_new = _gla_chunked(q_a, k_a, v_a, la, s_gla)
        o_b, s_ret_new = _ret_chunked(q_b, k_b, v_b, s_ret)
    else:
        o_a, s_gla_new = _gla_recurrent(q_a, k_a, v_a, la, s_gla)
        o_b, s_ret_new = _ret_recurrent(q_b, k_b, v_b, s_ret)
    o_a = o_a * lax.rsqrt(jnp.mean(o_a * o_a, axis=-1, keepdims=True) + EPS) * gla_norm_w.astype(F32)
    o_a = o_a.reshape(B, T, GLA_WIDTH) * jax.nn.silu(gg)
    mu = jnp.mean(o_b, axis=-1, keepdims=True)
    var = jnp.mean(jnp.square(o_b - mu), axis=-1, keepdims=True)
    o_b = ((o_b - mu) * lax.rsqrt(var + GN_EPS)).reshape(B, T, RET_WIDTH)
    o_b = (o_b * ret_norm_w.astype(F32) + ret_norm_b.astype(F32)) * jax.nn.silu(rg)
    mixed = jnp.concatenate([o_a, o_b], axis=-1).astype(x.dtype)
    y = x + jnp.einsum('btc,cd->btd', mixed, w_out)
    return y, s_gla_new, s_ret_new


def setup_inputs(seed: int = 0) -> dict:
    key = jax.random.key(seed)
    ks = jax.random.split(key, 14)
    nrm = jax.random.normal
    return {
        'x_prompt': nrm(ks[0], (BATCH, SEQ, D_MODEL), F32),
        'x_sample': nrm(ks[1], (DEC_BATCH, DEC_SEQ, D_MODEL), F32),
        'state_gla': nrm(ks[2], (DEPTH, DEC_BATCH, GLA_HEADS, GLA_DK, GLA_DV), F32),
        'state_ret': nrm(ks[3], (DEPTH, DEC_BATCH, RET_HEADS, RET_DK, RET_DV), F32),
        'norm_w': 1.0 + 0.05 * nrm(ks[4], (DEPTH, D_MODEL), F32),
        'w_in': nrm(ks[5], (DEPTH, D_MODEL, IN_COLS), F32) * D_MODEL ** -0.5,
        'gla_w2': nrm(ks[6], (DEPTH, GLA_RANK, GLA_HEADS * GLA_DK), F32) * GLA_RANK ** -0.5,
        'gla_b': 0.1 * nrm(ks[7], (DEPTH, GLA_HEADS * GLA_DK), F32),
        'gla_norm_w': 1.0 + 0.05 * nrm(ks[8], (DEPTH, GLA_DV), F32),
        'ret_norm_w': 1.0 + 0.05 * nrm(ks[9], (DEPTH, RET_WIDTH), F32),
        'ret_norm_b': 0.02 * nrm(ks[10], (DEPTH, RET_WIDTH), F32),
        'w_out': nrm(ks[11], (DEPTH, MIX_WIDTH, D_MODEL), F32) * MIX_WIDTH ** -0.5,
        'final_norm_w': 1.0 + 0.05 * nrm(ks[12], (D_MODEL,), F32),
    }


def reference(x_prompt, x_sample, state_gla, state_ret, norm_w, w_in, gla_w2, gla_b,
              gla_norm_w, ret_norm_w, ret_norm_b, w_out, final_norm_w):
    bp, tp, _ = x_prompt.shape
    bs, ts, _ = x_sample.shape
    pos_p = jnp.arange(tp, dtype=jnp.int32)
    pos_s = PAST_LEN + jnp.arange(ts, dtype=jnp.int32)
    hp, hs = x_prompt, x_sample
    gla_p, ret_p, gla_s, ret_s = [], [], [], []
    for l in range(DEPTH):
        w = (norm_w[l], w_in[l], gla_w2[l], gla_b[l], gla_norm_w[l], ret_norm_w[l], ret_norm_b[l], w_out[l])
        z_gla = jnp.zeros((bp, GLA_HEADS, GLA_DK, GLA_DV), F32)
        z_ret = jnp.zeros((bp, RET_HEADS, RET_DK, RET_DV), F32)
        hp, sg, sr = _layer(hp, pos_p, z_gla, z_ret, *w, chunked=True)
        gla_p.append(sg)
        ret_p.append(sr)
        hs, sg, sr = _layer(hs, pos_s, state_gla[l].astype(F32), state_ret[l].astype(F32), *w, chunked=False)
        gla_s.append(sg)
        ret_s.append(sr)
    y_prompt = _rmsnorm(hp, final_norm_w)
    y_sample = _rmsnorm(hs, final_norm_w)
    gla_state_prompt = jnp.stack(gla_p).astype(state_gla.dtype)
    ret_state_prompt = jnp.stack(ret_p).astype(state_ret.dtype)
    gla_state_sample = jnp.stack(gla_s).astype(state_gla.dtype)
    ret_state_sample = jnp.stack(ret_s).astype(state_ret.dtype)
    return (y_prompt, y_sample, gla_state_prompt, ret_state_prompt, gla_state_sample, ret_state_sample)
```

```python
import functools
import math

import numpy as np
import jax
import jax.numpy as jnp
from jax import lax
from jax.experimental import pallas as pl
from jax.experimental.pallas import tpu as pltpu

F32 = jnp.float32
BF16 = jnp.bfloat16

HEADS = 4
DK = 64
DV = 128
QK_W = HEADS * DK
V_W = HEADS * DV
RANK = 16
GATE_TAU = 16.0
CHUNK = 64
ROPE_BASE = 10000.0
PAST_LEN = 16384
EPS = 1e-6
GN_EPS = 1e-5
A_COLS = 2 * QK_W + 2 * V_W
LANES = 128
RANK_PAD = LANES
CUM_BLOCK = 256
VMEM_LIMIT_BYTES = 56 * 1024 * 1024

PROMPT_TILE = 512
SAMPLE_BLOCK = 8


def _dot(a, b):
    return jnp.dot(a, b, preferred_element_type=F32)


def _dot_nt(a, b):
    return lax.dot_general(a, b, (((1,), (1,)), ((), ())), preferred_element_type=F32)


def _dot_tn(a, b):
    return lax.dot_general(a, b, (((0,), (0,)), ((), ())), preferred_element_type=F32)


def _rmsnorm(x, w):
    return x * lax.rsqrt(jnp.mean(x * x, axis=-1, keepdims=True) + EPS) * w


def _silu(x):
    return x * (1.0 / (1.0 + jnp.exp(-x)))


def _log_sigmoid(x):
    return jnp.minimum(x, 0.0) - jnp.log1p(jnp.exp(-jnp.abs(x)))


def _rope(x, cos, sin_signed):
    lane = lax.broadcasted_iota(jnp.int32, (x.shape[0], LANES), 1)
    first_half = (lane % DK) < (DK // 2)
    outs = []
    for j in range(QK_W // LANES):
        xs = x[:, j * LANES:(j + 1) * LANES]
        swapped = jnp.where(first_half,
                            pltpu.roll(xs, LANES - DK // 2, axis=1),
                            pltpu.roll(xs, DK // 2, axis=1))
        outs.append(xs * cos + swapped * sin_signed)
    return jnp.concatenate(outs, axis=1)


def _split_hi_lo(x):
    hi = x.astype(BF16)
    lo = (x - hi.astype(F32)).astype(BF16)
    return hi, lo


def _col_broadcast(row):
    return jnp.broadcast_to(row, (LANES, row.shape[1])).T


def _head_norm_gate(oa_ref, ob_ref, ga_ref, gb_ref, gnw_ref, rnw_ref, rnb_ref):
    parts = []
    for j in range(HEADS):
        sl = slice(j * DV, (j + 1) * DV)
        o = oa_ref[:, sl]
        y = o * lax.rsqrt(jnp.mean(o * o, axis=-1, keepdims=True) + EPS) * gnw_ref[...]
        parts.append((y * ga_ref[:, sl]).astype(BF16))
    for j in range(HEADS):
        sl = slice(j * DV, (j + 1) * DV)
        o = ob_ref[:, sl]
        d = o - jnp.mean(o, axis=-1, keepdims=True)
        var = jnp.mean(d * d, axis=-1, keepdims=True)
        y = (d * lax.rsqrt(var + GN_EPS)) * rnw_ref[:, sl] + rnb_ref[:, sl]
        parts.append((y * gb_ref[:, sl]).astype(BF16))
    return jnp.concatenate(parts, axis=1)


def _project(hb, wa_ref, wlr_ref, wb_ref, w2_ref, gb_ref):
    gq = _dot(hb, wa_ref[:, 0:QK_W]) * (DK ** -0.5)
    gk = _dot(hb, wa_ref[:, QK_W:2 * QK_W])
    gv = _dot(hb, wa_ref[:, 2 * QK_W:2 * QK_W + V_W])
    gg = _dot(hb, wa_ref[:, 2 * QK_W + V_W:A_COLS])
    glr = _dot(hb, wlr_ref[...])
    la = _log_sigmoid(_dot(glr.astype(BF16), w2_ref[...]) + gb_ref[...]) * (1.0 / GATE_TAU)
    rq = _dot(hb, wb_ref[:, 0:QK_W])
    rk = _dot(hb, wb_ref[:, QK_W:2 * QK_W])
    rv = _dot(hb, wb_ref[:, 2 * QK_W:2 * QK_W + V_W])
    rg = _dot(hb, wb_ref[:, 2 * QK_W + V_W:A_COLS])
    return gq, gk, gv, gg, la, rq, rk, rv, rg


def _prompt_kernel(x_ref, nw_ref, wa_ref, wlr_ref, wb_ref, w2_ref, gb_ref, gnw_ref, rnw_ref,
                   rnb_ref, wo_ref, cos_ref, sin_ref, qdec_ref, kdec_ref, dmat_ref, fnw_ref,
                   y_ref, sg_ref, sr_ref,
                   qd_s, kd_s, ke_s, el_s, va_s, ga_s, qb_s, qbd_s, kb_s, kbe_s, vb_s, gbt_s,
                   oa_s, ob_s, stg_s, str_s, kbdg_s, vbdg_s, kbdr_s, vbdr_s, sbdg_s, sbdr_s,
                   *, tile, final_norm):
    t = pl.program_id(1)

    @pl.when(t == 0)
    def _init():
        stg_s[...] = jnp.zeros_like(stg_s)
        str_s[...] = jnp.zeros_like(str_s)
        for buf in (kbdg_s, vbdg_s, kbdr_s, vbdr_s):
            buf[...] = jnp.zeros_like(buf)
        sbdg_s[...] = jnp.zeros_like(sbdg_s)
        sbdr_s[...] = jnp.zeros_like(sbdr_s)

    x = x_ref[0]
    hb = _rmsnorm(x, nw_ref[...]).astype(BF16)
    gq, gk, gv, gg, la, rq, rk, rv, rg = _project(hb, wa_ref, wlr_ref, wb_ref, w2_ref, gb_ref)

    va_s[...] = gv.astype(BF16)
    ga_s[...] = _silu(gg)
    vb_s[...] = rv.astype(BF16)
    gbt_s[...] = _silu(rg)

    row = lax.broadcasted_iota(jnp.int32, (2 * CUM_BLOCK, CUM_BLOCK), 0)
    col = lax.broadcasted_iota(jnp.int32, (2 * CUM_BLOCK, CUM_BLOCK), 1)
    is_cum = row < CUM_BLOCK
    r = jnp.where(is_cum, row, row - CUM_BLOCK)
    same_chunk = (r // CHUNK) == (col // CHUNK)
    cum_lhs = jnp.where(same_chunk & (jnp.logical_not(is_cum) | (col <= r)), 1.0, 0.0).astype(BF16)
    for i in range(tile // CUM_BLOCK):
        rows = slice(i * CUM_BLOCK, (i + 1) * CUM_BLOCK)
        hi, lo = _split_hi_lo(la[rows])
        both = _dot(cum_lhs, hi) + _dot(cum_lhs, lo)
        b = both[:CUM_BLOCK]
        bl = both[CUM_BLOCK:]
        q, k = gq[rows], gk[rows]
        qd_s[rows, :] = (q * jnp.exp(b)).astype(BF16)
        kd_s[rows, :] = (k * jnp.exp(-b)).astype(BF16)
        ke_s[rows, :] = k * jnp.exp(bl - b)
        el_s[rows, :] = jnp.exp(bl)

    cos = cos_ref[...]
    sin = sin_ref[...]
    qb = _rope(rq, cos, sin)
    kb = _rope(rk, cos, sin) * (DK ** -0.5)
    qb_s[...] = qb.astype(BF16)
    qbd_s[...] = (qb * qdec_ref[...]).astype(BF16)
    kb_s[...] = kb.astype(BF16)
    kbe_s[...] = kb * kdec_ref[...]

    tril = dmat_ref[...] > 0.0

    def block_diag_scores(qs, ks, kbd_s):
        for h in range(HEADS):
            kbd_s[h * CHUNK:(h + 1) * CHUNK, h * DK:(h + 1) * DK] = ks[:, h * DK:(h + 1) * DK]
        return _dot_nt(qs, kbd_s[...])

    def block_diag_values(vs, vbd_s):
        for h in range(HEADS):
            vbd_s[h * CHUNK:(h + 1) * CHUNK, h * DV:(h + 1) * DV] = vs[:, h * DV:(h + 1) * DV]
        return vbd_s[...]

    def chunk_body(c, carry):
        rows = pl.ds(pl.multiple_of(c * CHUNK, CHUNK), CHUNK)
        qd = qd_s[rows, :]
        v = va_s[rows, :]
        att = jnp.where(tril, block_diag_scores(qd, kd_s[rows, :], kbdg_s), 0.0).astype(BF16)
        oa_s[rows, :] = _dot(att, block_diag_values(v, vbdg_s)) + _dot(qd, sbdg_s[...])
        ecol = _col_broadcast(el_s[pl.ds(c * CHUNK, 1), :])
        ke = ke_s[rows, :].astype(BF16)
        for h in range(HEADS):
            kv = _dot_tn(ke[:, h * DK:(h + 1) * DK], v[:, h * DV:(h + 1) * DV])
            s_new = ecol[h * DK:(h + 1) * DK] * stg_s[h] + kv
            stg_s[h] = s_new
            sbdg_s[h * DK:(h + 1) * DK, h * DV:(h + 1) * DV] = s_new.astype(BF16)
        qr = qb_s[rows, :]
        vr = vb_s[rows, :]
        attr = (block_diag_scores(qr, kb_s[rows, :], kbdr_s) * dmat_ref[...]).astype(BF16)
        ob_s[rows, :] = _dot(attr, block_diag_values(vr, vbdr_s)) + _dot(qbd_s[rows, :], sbdr_s[...])
        kbe = kbe_s[rows, :].astype(BF16)
        for h in range(HEADS):
            kv = _dot_tn(kbe[:, h * DK:(h + 1) * DK], vr[:, h * DV:(h + 1) * DV])
            s_new = _ret_gamma(h) ** CHUNK * str_s[h] + kv
            str_s[h] = s_new
            sbdr_s[h * DK:(h + 1) * DK, h * DV:(h + 1) * DV] = s_new.astype(BF16)
        return carry

    lax.fori_loop(0, tile // CHUNK, chunk_body, 0)

    mixed = _head_norm_gate(oa_s, ob_s, ga_s, gbt_s, gnw_ref, rnw_ref, rnb_ref)
    y = x + _dot(mixed, wo_ref[...])
    if final_norm:
        y = _rmsnorm(y, fnw_ref[...])
    y_ref[0] = y

    @pl.when(t == pl.num_programs(1) - 1)
    def _emit_state():
        sg_ref[0] = stg_s[...]
        sr_ref[0] = str_s[...]


def _ret_gamma(h):
    return 1.0 - 2.0 ** (-5.0 - h)


def _const_spec(shape):
    nd = len(shape)
    return pl.BlockSpec(shape, lambda *_: (0,) * nd)


def _prompt_layer(x, wts, tables, fnw, *, final_norm):
    bsz, seq, d = x.shape
    tile = min(PROMPT_TILE, seq)
    assert seq % tile == 0 and tile % CUM_BLOCK == 0
    nt = seq // tile
    cos, sin, qdec, kdec, dmat = tables
    consts = list(wts) + [cos, sin, qdec, kdec, dmat, fnw]
    in_specs = [pl.BlockSpec((1, tile, d), lambda b, t: (b, t, 0))]
    for a in wts:
        in_specs.append(_const_spec(a.shape))
    in_specs += [pl.BlockSpec((tile, LANES), lambda b, t: (t, 0)),
                 pl.BlockSpec((tile, LANES), lambda b, t: (t, 0)),
                 _const_spec(qdec.shape), _const_spec(kdec.shape), _const_spec(dmat.shape),
                 _const_spec(fnw.shape)]
    state_shape = jax.ShapeDtypeStruct((bsz, HEADS, DK, DV), F32)
    state_spec = pl.BlockSpec((1, HEADS, DK, DV), lambda b, t: (b, 0, 0, 0))
    scratch = [
        pltpu.VMEM((tile, QK_W), BF16),
        pltpu.VMEM((tile, QK_W), BF16),
        pltpu.VMEM((tile, QK_W), F32),
        pltpu.VMEM((tile, QK_W), F32),
        pltpu.VMEM((tile, V_W), BF16),
        pltpu.VMEM((tile, V_W), F32),
        pltpu.VMEM((tile, QK_W), BF16),
        pltpu.VMEM((tile, QK_W), BF16),
        pltpu.VMEM((tile, QK_W), BF16),
        pltpu.VMEM((tile, QK_W), F32),
        pltpu.VMEM((tile, V_W), BF16),
        pltpu.VMEM((tile, V_W), F32),
        pltpu.VMEM((tile, V_W), F32),
        pltpu.VMEM((tile, V_W), F32),
        pltpu.VMEM((HEADS, DK, DV), F32),
        pltpu.VMEM((HEADS, DK, DV), F32),
        pltpu.VMEM((HEADS * CHUNK, QK_W), BF16),
        pltpu.VMEM((HEADS * CHUNK, V_W), BF16),
        pltpu.VMEM((HEADS * CHUNK, QK_W), BF16),
        pltpu.VMEM((HEADS * CHUNK, V_W), BF16),
        pltpu.VMEM((QK_W, V_W), BF16),
        pltpu.VMEM((QK_W, V_W), BF16),
    ]
    return pl.pallas_call(
        functools.partial(_prompt_kernel, tile=tile, final_norm=final_norm),
        grid=(bsz, nt),
        in_specs=in_specs,
        out_specs=[pl.BlockSpec((1, tile, d), lambda b, t: (b, t, 0)), state_spec, state_spec],
        out_shape=[jax.ShapeDtypeStruct(x.shape, F32), state_shape, state_shape],
        scratch_shapes=scratch,
        compiler_params=pltpu.CompilerParams(
            dimension_semantics=("arbitrary", "arbitrary"),
            vmem_limit_bytes=VMEM_LIMIT_BYTES),
        name="prompt_layer",
    )(x, *consts)


def _sample_kernel(x_ref, nw_ref, wa_ref, wlr_ref, wb_ref, w2_ref, gb_ref, gnw_ref, rnw_ref,
                   rnb_ref, wo_ref, cos_ref, sin_ref, fnw_ref, sg_in, sr_in,
                   y_ref, sg_out, sr_out,
                   qa_s, ka_s, al_s, va_s, ga_s, qb_s, kb_s, vb_s, gbt_s, oa_s, ob_s,
                   *, block, final_norm):
    i = pl.program_id(0)

    @pl.when(i == 0)
    def _project_all():
        hb = _rmsnorm(x_ref[...], nw_ref[...]).astype(BF16)
        gq, gk, gv, gg, la, rq, rk, rv, rg = _project(hb, wa_ref, wlr_ref, wb_ref, w2_ref, gb_ref)
        qa_s[...] = gq
        ka_s[...] = gk
        al_s[...] = jnp.exp(la)
        va_s[...] = gv
        ga_s[...] = _silu(gg)
        qb_s[...] = _rope(rq, cos_ref[...], sin_ref[...])
        kb_s[...] = _rope(rk, cos_ref[...], sin_ref[...]) * (DK ** -0.5)
        vb_s[...] = rv
        gbt_s[...] = _silu(rg)

    rows = pl.ds(pl.multiple_of(i * block, block), block)
    qa, ka, al, va = qa_s[rows, :], ka_s[rows, :], al_s[rows, :], va_s[rows, :]
    qb, kb, vb = qb_s[rows, :], kb_s[rows, :], vb_s[rows, :]
    oa_rows, ob_rows = [], []
    for j in range(block):
        one = slice(j, j + 1)
        qcol = _col_broadcast(qa[one])
        kcol = _col_broadcast(ka[one])
        acol = _col_broadcast(al[one])
        qrc = _col_broadcast(qb[one])
        krc = _col_broadcast(kb[one])
        oa_h, ob_h = [], []
        for h in range(HEADS):
            ks = slice(h * DK, (h + 1) * DK)
            vs = slice(h * DV, (h + 1) * DV)
            s_new = acol[ks] * sg_in[j, h] + kcol[ks] * va[one, vs]
            sg_out[j, h] = s_new
            oa_h.append(jnp.sum(qcol[ks] * s_new, axis=0, keepdims=True))
            r_new = _ret_gamma(h) * sr_in[j, h] + krc[ks] * vb[one, vs]
            sr_out[j, h] = r_new
            ob_h.append(jnp.sum(qrc[ks] * r_new, axis=0, keepdims=True))
        oa_rows.append(jnp.concatenate(oa_h, axis=1))
        ob_rows.append(jnp.concatenate(ob_h, axis=1))
    oa_s[rows, :] = jnp.concatenate(oa_rows, axis=0)
    ob_s[rows, :] = jnp.concatenate(ob_rows, axis=0)

    @pl.when(i == pl.num_programs(0) - 1)
    def _finish():
        mixed = _head_norm_gate(oa_s, ob_s, ga_s, gbt_s, gnw_ref, rnw_ref, rnb_ref)
        y = x_ref[...] + _dot(mixed, wo_ref[...])
        if final_norm:
            y = _rmsnorm(y, fnw_ref[...])
        y_ref[...] = y


def _sample_layer(x, wts, cos, sin, fnw, sg, sr, *, layer, final_norm):
    n, d = x.shape
    block = SAMPLE_BLOCK
    assert n % block == 0
    consts = list(wts) + [cos, sin, fnw]
    in_specs = [_const_spec(x.shape)] + [_const_spec(a.shape) for a in consts]
    st_spec = pl.BlockSpec((block, HEADS, DK, DV), lambda i: (i, 0, 0, 0))
    st_in_spec = pl.BlockSpec((None, block, HEADS, DK, DV), lambda i: (layer, i, 0, 0, 0))
    in_specs += [st_in_spec, st_in_spec]
    scratch = [pltpu.VMEM((n, QK_W), F32)] * 3 + [pltpu.VMEM((n, V_W), F32)] * 2 \
        + [pltpu.VMEM((n, QK_W), F32)] * 2 + [pltpu.VMEM((n, V_W), F32)] * 2 \
        + [pltpu.VMEM((n, V_W), F32)] * 2
    return pl.pallas_call(
        functools.partial(_sample_kernel, block=block, final_norm=final_norm),
        grid=(n // block,),
        in_specs=in_specs,
        out_specs=[_const_spec(x.shape), st_spec, st_spec],
        out_shape=[jax.ShapeDtypeStruct(x.shape, F32),
                   jax.ShapeDtypeStruct(sg.shape[1:], F32), jax.ShapeDtypeStruct(sr.shape[1:], F32)],
        scratch_shapes=scratch,
        compiler_params=pltpu.CompilerParams(
            dimension_semantics=("arbitrary",), vmem_limit_bytes=VMEM_LIMIT_BYTES),
        name="sample_layer",
    )(x, *consts, sg, sr)


def _rope_tables(pos):
    half = DK // 2
    inv = ROPE_BASE ** (-np.arange(half, dtype=np.float64) / half)
    ang = np.asarray(pos, dtype=np.float64)[:, None] * inv[None, :]
    cos = np.tile(np.cos(ang), (1, LANES // half))
    sin = np.tile(np.concatenate([-np.sin(ang), np.sin(ang)], axis=1), (1, LANES // DK))
    return jnp.asarray(cos, F32), jnp.asarray(sin, F32)


def _retention_tables(tile):
    idx = np.arange(CHUNK, dtype=np.float64)
    gam = np.array([_ret_gamma(h) for h in range(HEADS)], dtype=np.float64)
    qdec = np.repeat(gam[None, :] ** (idx[:, None] + 1.0), DK, axis=1)
    kdec = np.repeat(gam[None, :] ** (CHUNK - 1.0 - idx[:, None]), DK, axis=1)
    rel = idx[:, None] - idx[None, :]
    dm = [np.where(rel >= 0, gam[h] ** np.maximum(rel, 0.0), 0.0) for h in range(HEADS)]
    dmat = np.concatenate(dm, axis=1)
    reps = tile // CHUNK
    return (jnp.asarray(np.tile(qdec, (reps, 1)), F32), jnp.asarray(np.tile(kdec, (reps, 1)), F32),
            jnp.asarray(dmat, F32))


def _layer_weights(l, norm_w, w_in, gla_w2, gla_b, gla_norm_w, ret_norm_w, ret_norm_b, w_out):
    wi = w_in[l]
    lr0 = A_COLS
    wa = wi[:, :lr0].astype(BF16)
    wlr = jnp.pad(wi[:, lr0:lr0 + RANK], ((0, 0), (0, RANK_PAD - RANK))).astype(BF16)
    wb = wi[:, lr0 + RANK:].astype(BF16)
    w2 = jnp.pad(gla_w2[l], ((0, RANK_PAD - RANK), (0, 0))).astype(BF16)
    return (norm_w[l][None, :], wa, wlr, wb, w2, gla_b[l][None, :], gla_norm_w[l][None, :],
            ret_norm_w[l][None, :], ret_norm_b[l][None, :], w_out[l].astype(BF16))


def kernel(x_prompt, x_sample, state_gla, state_ret, norm_w, w_in, gla_w2, gla_b, gla_norm_w,
           ret_norm_w, ret_norm_b, w_out, final_norm_w):
    bp, tp, d = x_prompt.shape
    bs, ts, _ = x_sample.shape
    assert ts == 1, "the decode path handles one new token per sequence"
    depth = w_in.shape[0]
    tile = min(PROMPT_TILE, tp)
    cos_p, sin_p = _rope_tables(np.arange(tp))
    cos_s, sin_s = _rope_tables(PAST_LEN + np.arange(ts))
    ret_tabs = _retention_tables(tile)
    fnw = final_norm_w[None, :]

    hp = x_prompt
    hs = x_sample.reshape(bs, d)
    gla_p, ret_p, gla_s, ret_s = [], [], [], []
    for l in range(depth):
        wts = _layer_weights(l, norm_w, w_in, gla_w2, gla_b, gla_norm_w, ret_norm_w, ret_norm_b,
                             w_out)
        last = l == depth - 1
        hp, sg, sr = _prompt_layer(hp, wts, (cos_p, sin_p) + ret_tabs, fnw, final_norm=last)
        gla_p.append(sg)
        ret_p.append(sr)
        hs, sg, sr = _sample_layer(hs, wts, cos_s, sin_s, fnw, state_gla, state_ret,
                                   layer=l, final_norm=last)
        gla_s.append(sg)
        ret_s.append(sr)
    return (hp, hs.reshape(bs, ts, d), jnp.stack(gla_p), jnp.stack(ret_p),
            jnp.stack(gla_s), jnp.stack(ret_s))
```

```python
import functools

import numpy as np
import jax
import jax.numpy as jnp
from jax import lax
from jax.experimental import pallas as pl
from jax.experimental.pallas import tpu as pltpu

F32 = jnp.float32
BF16 = jnp.bfloat16

HEADS = 4
DK = 64
DV = 128
QK_W = HEADS * DK
V_W = HEADS * DV
RANK = 16
GATE_TAU = 16.0
CHUNK = 64
ROPE_BASE = 10000.0
PAST_LEN = 16384
EPS = 1e-6
GN_EPS = 1e-5
A_COLS = 2 * QK_W + 2 * V_W
LANES = 128
RANK_PAD = LANES
CUM_BLOCK = 256
VMEM_LIMIT_BYTES = 56 * 1024 * 1024

PROMPT_TILE = 512
SAMPLE_BLOCK = 8


def _dot(a, b):
    return jnp.dot(a, b, preferred_element_type=F32)


def _dot_nt(a, b):
    return lax.dot_general(a, b, (((1,), (1,)), ((), ())), preferred_element_type=F32)


def _dot_tn(a, b):
    return lax.dot_general(a, b, (((0,), (0,)), ((), ())), preferred_element_type=F32)


def _rmsnorm(x, w):
    return x * lax.rsqrt(jnp.mean(x * x, axis=-1, keepdims=True) + EPS) * w


def _silu(x):
    return x * (1.0 / (1.0 + jnp.exp(-x)))


def _log_sigmoid(x):
    return jnp.minimum(x, 0.0) - jnp.log1p(jnp.exp(-jnp.abs(x)))


def _ret_gamma(h):
    return 1.0 - 2.0 ** (-5.0 - h)


def _rope(x, cos, sin_signed):
    lane = lax.broadcasted_iota(jnp.int32, (x.shape[0], LANES), 1)
    first_half = (lane % DK) < (DK // 2)
    outs = []
    for j in range(QK_W // LANES):
        xs = x[:, j * LANES:(j + 1) * LANES]
        swapped = jnp.where(first_half,
                            pltpu.roll(xs, LANES - DK // 2, axis=1),
                            pltpu.roll(xs, DK // 2, axis=1))
        outs.append(xs * cos + swapped * sin_signed)
    return jnp.concatenate(outs, axis=1)


def _split_hi_lo(x):
    hi = x.astype(BF16)
    lo = (x - hi.astype(F32)).astype(BF16)
    return hi, lo


def _col_broadcast(row):
    return jnp.broadcast_to(row, (LANES, row.shape[1])).T


def _block_diag_keys(k, head_of_lane):
    return jnp.concatenate(
        [jnp.where(head_of_lane == h, k, jnp.zeros_like(k)) for h in range(HEADS)], axis=0)


def _block_diag_values(blocks):
    z = jnp.zeros_like(blocks[0])
    return jnp.concatenate(
        [jnp.concatenate([blocks[h] if j == h else z for j in range(HEADS)], axis=1)
         for h in range(HEADS)], axis=0)


def _head_blocks(v):
    return [v[:, h * DV:(h + 1) * DV] for h in range(HEADS)]


def _head_norm_gate(oa_ref, ob_ref, ga_ref, gb_ref, gnw_ref, rnw_ref, rnb_ref):
    parts = []
    for j in range(HEADS):
        sl = slice(j * DV, (j + 1) * DV)
        o = oa_ref[:, sl]
        y = o * lax.rsqrt(jnp.mean(o * o, axis=-1, keepdims=True) + EPS) * gnw_ref[...]
        parts.append((y * ga_ref[:, sl]).astype(BF16))
    for j in range(HEADS):
        sl = slice(j * DV, (j + 1) * DV)
        o = ob_ref[:, sl]
        d = o - jnp.mean(o, axis=-1, keepdims=True)
        var = jnp.mean(d * d, axis=-1, keepdims=True)
        y = (d * lax.rsqrt(var + GN_EPS)) * rnw_ref[:, sl] + rnb_ref[:, sl]
        parts.append((y * gb_ref[:, sl]).astype(BF16))
    return jnp.concatenate(parts, axis=1)


def _project(hb, wa_ref, wlr_ref, wb_ref, w2_ref, gb_ref):
    gq = _dot(hb, wa_ref[:, 0:QK_W]) * (DK ** -0.5)
    gk = _dot(hb, wa_ref[:, QK_W:2 * QK_W])
    gv = _dot(hb, wa_ref[:, 2 * QK_W:2 * QK_W + V_W])
    gg = _dot(hb, wa_ref[:, 2 * QK_W + V_W:A_COLS])
    glr = _dot(hb, wlr_ref[...])
    la = _log_sigmoid(_dot(glr.astype(BF16), w2_ref[...]) + gb_ref[...]) * (1.0 / GATE_TAU)
    rq = _dot(hb, wb_ref[:, 0:QK_W])
    rk = _dot(hb, wb_ref[:, QK_W:2 * QK_W])
    rv = _dot(hb, wb_ref[:, 2 * QK_W:2 * QK_W + V_W])
    rg = _dot(hb, wb_ref[:, 2 * QK_W + V_W:A_COLS])
    return gq, gk, gv, gg, la, rq, rk, rv, rg


def _prompt_kernel(x_ref, nw_ref, wa_ref, wlr_ref, wb_ref, w2_ref, gb_ref, gnw_ref, rnw_ref,
                   rnb_ref, wo_ref, cos_ref, sin_ref, qdec_ref, kdec_ref, dmat_ref, fnw_ref,
                   y_ref, sg_ref, sr_ref,
                   qd_s, kd_s, ke_s, va_s, ga_s, qb_s, qbd_s, kb_s, kbe_s, vb_s, gbt_s,
                   oa_s, ob_s, stg_s, str_s,
                   *, tile, final_norm):
    t = pl.program_id(1)

    @pl.when(t == 0)
    def _init():
        stg_s[...] = jnp.zeros_like(stg_s)
        str_s[...] = jnp.zeros_like(str_s)

    x = x_ref[0]
    hb = _rmsnorm(x, nw_ref[...]).astype(BF16)
    gq, gk, gv, gg, la, rq, rk, rv, rg = _project(hb, wa_ref, wlr_ref, wb_ref, w2_ref, gb_ref)

    va_s[...] = gv.astype(BF16)
    ga_s[...] = _silu(gg)
    vb_s[...] = rv.astype(BF16)
    gbt_s[...] = _silu(rg)

    row = lax.broadcasted_iota(jnp.int32, (CUM_BLOCK, CUM_BLOCK), 0)
    col = lax.broadcasted_iota(jnp.int32, (CUM_BLOCK, CUM_BLOCK), 1)
    cum_lhs = jnp.where(((row // CHUNK) == (col // CHUNK)) & (col <= row), 1.0, 0.0).astype(BF16)
    chunk_decay_rows = []
    for i in range(tile // CUM_BLOCK):
        rows = slice(i * CUM_BLOCK, (i + 1) * CUM_BLOCK)
        hi, lo = _split_hi_lo(la[rows])
        b = _dot(cum_lhs, hi) + _dot(cum_lhs, lo)
        b_last = []
        for c in range(CUM_BLOCK // CHUNK):
            last = b[(c + 1) * CHUNK - 1:(c + 1) * CHUNK]
            chunk_decay_rows.append(jnp.exp(last))
            b_last.append(jnp.broadcast_to(last, (CHUNK, QK_W)))
        bl = jnp.concatenate(b_last, axis=0)
        q, k = gq[rows], gk[rows]
        qd_s[rows, :] = (q * jnp.exp(b)).astype(BF16)
        kd_s[rows, :] = (k * jnp.exp(-b)).astype(BF16)
        ke_s[rows, :] = (k * jnp.exp(bl - b)).astype(BF16)

    cos = cos_ref[...]
    sin = sin_ref[...]
    qb = _rope(rq, cos, sin)
    kb = _rope(rk, cos, sin) * (DK ** -0.5)
    qb_s[...] = qb.astype(BF16)
    qbd_s[...] = (qb * qdec_ref[...]).astype(BF16)
    kb_s[...] = kb.astype(BF16)
    kbe_s[...] = (kb * kdec_ref[...]).astype(BF16)

    dmat = dmat_ref[...]
    tril = dmat > 0.0
    head_of_lane = lax.broadcasted_iota(jnp.int32, (CHUNK, QK_W), 1) // DK
    n_chunks = tile // CHUNK
    chunk_rows = [slice(c * CHUNK, (c + 1) * CHUNK) for c in range(n_chunks)]
    att_gla, att_ret = [], []
    for rows in chunk_rows:
        scores = _dot_nt(qd_s[rows, :], _block_diag_keys(kd_s[rows, :], head_of_lane))
        att_gla.append(jnp.where(tril, scores, 0.0).astype(BF16))
        scores = _dot_nt(qb_s[rows, :], _block_diag_keys(kb_s[rows, :], head_of_lane))
        att_ret.append((scores * dmat).astype(BF16))
    kv_gla, kv_ret = [], []
    for rows in chunk_rows:
        ke, v = ke_s[rows, :], _head_blocks(va_s[rows, :])
        kv_gla.append([_dot_tn(ke[:, h * DK:(h + 1) * DK], v[h]) for h in range(HEADS)])
        kbe, vr = kbe_s[rows, :], _head_blocks(vb_s[rows, :])
        kv_ret.append([_dot_tn(kbe[:, h * DK:(h + 1) * DK], vr[h]) for h in range(HEADS)])
    s_gla = [stg_s[h] for h in range(HEADS)]
    s_ret = [str_s[h] for h in range(HEADS)]
    sin_gla, sin_ret = [], []
    for c in range(n_chunks):
        sin_gla.append([s.astype(BF16) for s in s_gla])
        sin_ret.append([s.astype(BF16) for s in s_ret])
        ecol = _col_broadcast(chunk_decay_rows[c])
        s_gla = [ecol[h * DK:(h + 1) * DK] * s_gla[h] + kv_gla[c][h] for h in range(HEADS)]
        s_ret = [_ret_gamma(h) ** CHUNK * s_ret[h] + kv_ret[c][h] for h in range(HEADS)]
    for h in range(HEADS):
        stg_s[h] = s_gla[h]
        str_s[h] = s_ret[h]
    for c, rows in enumerate(chunk_rows):
        oa_s[rows, :] = (_dot(att_gla[c], _block_diag_values(_head_blocks(va_s[rows, :])))
                         + _dot(qd_s[rows, :], _block_diag_values(sin_gla[c])))
        ob_s[rows, :] = (_dot(att_ret[c], _block_diag_values(_head_blocks(vb_s[rows, :])))
                         + _dot(qbd_s[rows, :], _block_diag_values(sin_ret[c])))

    mixed = _head_norm_gate(oa_s, ob_s, ga_s, gbt_s, gnw_ref, rnw_ref, rnb_ref)
    y = x + _dot(mixed, wo_ref[...])
    if final_norm:
        y = _rmsnorm(y, fnw_ref[...])
    y_ref[0] = y

    @pl.when(t == pl.num_programs(1) - 1)
    def _emit_state():
        sg_ref[0] = stg_s[...]
        sr_ref[0] = str_s[...]


def _const_spec(shape):
    nd = len(shape)
    return pl.BlockSpec(shape, lambda *_: (0,) * nd)


def _prompt_layer(x, wts, tables, fnw, *, final_norm):
    bsz, seq, d = x.shape
    tile = min(PROMPT_TILE, seq)
    assert seq % tile == 0 and tile % CUM_BLOCK == 0
    nt = seq // tile
    cos, sin, qdec, kdec, dmat = tables
    consts = list(wts) + [cos, sin, qdec, kdec, dmat, fnw]
    in_specs = [pl.BlockSpec((1, tile, d), lambda b, t: (b, t, 0))]
    for a in wts:
        in_specs.append(_const_spec(a.shape))
    in_specs += [pl.BlockSpec((tile, LANES), lambda b, t: (t, 0)),
                 pl.BlockSpec((tile, LANES), lambda b, t: (t, 0)),
                 _const_spec(qdec.shape), _const_spec(kdec.shape), _const_spec(dmat.shape),
                 _const_spec(fnw.shape)]
    state_shape = jax.ShapeDtypeStruct((bsz, HEADS, DK, DV), F32)
    state_spec = pl.BlockSpec((1, HEADS, DK, DV), lambda b, t: (b, 0, 0, 0))
    scratch = [
        pltpu.VMEM((tile, QK_W), BF16),
        pltpu.VMEM((tile, QK_W), BF16),
        pltpu.VMEM((tile, QK_W), BF16),
        pltpu.VMEM((tile, V_W), BF16),
        pltpu.VMEM((tile, V_W), F32),
        pltpu.VMEM((tile, QK_W), BF16),
        pltpu.VMEM((tile, QK_W), BF16),
        pltpu.VMEM((tile, QK_W), BF16),
        pltpu.VMEM((tile, QK_W), BF16),
        pltpu.VMEM((tile, V_W), BF16),
        pltpu.VMEM((tile, V_W), F32),
        pltpu.VMEM((tile, V_W), F32),
        pltpu.VMEM((tile, V_W), F32),
        pltpu.VMEM((HEADS, DK, DV), F32),
        pltpu.VMEM((HEADS, DK, DV), F32),
    ]
    return pl.pallas_call(
        functools.partial(_prompt_kernel, tile=tile, final_norm=final_norm),
        grid=(bsz, nt),
        in_specs=in_specs,
        out_specs=[pl.BlockSpec((1, tile, d), lambda b, t: (b, t, 0)), state_spec, state_spec],
        out_shape=[jax.ShapeDtypeStruct(x.shape, F32), state_shape, state_shape],
        scratch_shapes=scratch,
        compiler_params=pltpu.CompilerParams(
            dimension_semantics=("arbitrary", "arbitrary"),
            vmem_limit_bytes=VMEM_LIMIT_BYTES),
        name="prompt_layer",
    )(x, *consts)


def _sample_kernel(x_ref, nw_ref, wa_ref, wlr_ref, wb_ref, w2_ref, gb_ref, gnw_ref, rnw_ref,
                   rnb_ref, wo_ref, cos_ref, sin_ref, fnw_ref, sg_in, sr_in, *rest,
                   block, final_norm, chained):
    if chained:
        rest = rest[2:]
    (y_ref, sg_out, sr_out,
     qa_s, ka_s, al_s, va_s, ga_s, qb_s, kb_s, vb_s, gbt_s, oa_s, ob_s) = rest
    i = pl.program_id(0)

    @pl.when(i == 0)
    def _project_all():
        hb = _rmsnorm(x_ref[...], nw_ref[...]).astype(BF16)
        gq, gk, gv, gg, la, rq, rk, rv, rg = _project(hb, wa_ref, wlr_ref, wb_ref, w2_ref, gb_ref)
        qa_s[...] = gq
        ka_s[...] = gk
        al_s[...] = jnp.exp(la)
        va_s[...] = gv
        ga_s[...] = _silu(gg)
        qb_s[...] = _rope(rq, cos_ref[...], sin_ref[...])
        kb_s[...] = _rope(rk, cos_ref[...], sin_ref[...]) * (DK ** -0.5)
        vb_s[...] = rv
        gbt_s[...] = _silu(rg)

    rows = pl.ds(pl.multiple_of(i * block, block), block)
    qa, ka, al, va = qa_s[rows, :], ka_s[rows, :], al_s[rows, :], va_s[rows, :]
    qb, kb, vb = qb_s[rows, :], kb_s[rows, :], vb_s[rows, :]
    oa_rows, ob_rows = [], []
    for j in range(block):
        one = slice(j, j + 1)
        qcol = _col_broadcast(qa[one])
        kcol = _col_broadcast(ka[one])
        acol = _col_broadcast(al[one])
        qrc = _col_broadcast(qb[one])
        krc = _col_broadcast(kb[one])
        oa_h, ob_h = [], []
        for h in range(HEADS):
            ks = slice(h * DK, (h + 1) * DK)
            vs = slice(h * DV, (h + 1) * DV)
            s_new = acol[ks] * sg_in[j, h] + kcol[ks] * va[one, vs]
            sg_out[j, h] = s_new
            oa_h.append(jnp.sum(qcol[ks] * s_new, axis=0, keepdims=True))
            r_new = _ret_gamma(h) * sr_in[j, h] + krc[ks] * vb[one, vs]
            sr_out[j, h] = r_new
            ob_h.append(jnp.sum(qrc[ks] * r_new, axis=0, keepdims=True))
        oa_rows.append(jnp.concatenate(oa_h, axis=1))
        ob_rows.append(jnp.concatenate(ob_h, axis=1))
    oa_s[rows, :] = jnp.concatenate(oa_rows, axis=0)
    ob_s[rows, :] = jnp.concatenate(ob_rows, axis=0)

    @pl.when(i == pl.num_programs(0) - 1)
    def _finish():
        mixed = _head_norm_gate(oa_s, ob_s, ga_s, gbt_s, gnw_ref, rnw_ref, rnb_ref)
        y = x_ref[...] + _dot(mixed, wo_ref[...])
        if final_norm:
            y = _rmsnorm(y, fnw_ref[...])
        y_ref[...] = y


def _sample_layer(x, wts, cos, sin, fnw, sg, sr, prev_out, *, layer, final_norm):
    n, d = x.shape
    block = SAMPLE_BLOCK
    assert n % block == 0
    consts = list(wts) + [cos, sin, fnw]
    in_specs = [_const_spec(x.shape)] + [_const_spec(a.shape) for a in consts]
    st_spec = pl.BlockSpec((None, block, HEADS, DK, DV), lambda i: (layer, i, 0, 0, 0))
    in_specs += [st_spec, st_spec]
    operands = [x, *consts, sg, sr]
    aliases = {}
    if prev_out is not None:
        in_specs += [pl.BlockSpec(memory_space=pl.ANY)] * 2
        aliases = {len(operands): 1, len(operands) + 1: 2}
        operands += list(prev_out)
    scratch = [pltpu.VMEM((n, QK_W), F32)] * 3 + [pltpu.VMEM((n, V_W), F32)] * 2 \
        + [pltpu.VMEM((n, QK_W), F32)] * 2 + [pltpu.VMEM((n, V_W), F32)] * 2 \
        + [pltpu.VMEM((n, V_W), F32)] * 2
    y, sg_new, sr_new = pl.pallas_call(
        functools.partial(_sample_kernel, block=block, final_norm=final_norm,
                          chained=prev_out is not None),
        grid=(n // block,),
        in_specs=in_specs,
        out_specs=[_const_spec(x.shape), st_spec, st_spec],
        out_shape=[jax.ShapeDtypeStruct(x.shape, F32),
                   jax.ShapeDtypeStruct(sg.shape, F32), jax.ShapeDtypeStruct(sr.shape, F32)],
        input_output_aliases=aliases,
        scratch_shapes=scratch,
        compiler_params=pltpu.CompilerParams(
            dimension_semantics=("arbitrary",), vmem_limit_bytes=VMEM_LIMIT_BYTES),
        name="sample_layer",
    )(*operands)
    return y, (sg_new, sr_new)


def _rope_tables(pos):
    half = DK // 2
    inv = ROPE_BASE ** (-np.arange(half, dtype=np.float64) / half)
    ang = np.asarray(pos, dtype=np.float64)[:, None] * inv[None, :]
    cos = np.tile(np.cos(ang), (1, LANES // half))
    sin = np.tile(np.concatenate([-np.sin(ang), np.sin(ang)], axis=1), (1, LANES // DK))
    return jnp.asarray(cos, F32), jnp.asarray(sin, F32)


def _retention_tables(tile):
    idx = np.arange(CHUNK, dtype=np.float64)
    gam = np.array([_ret_gamma(h) for h in range(HEADS)], dtype=np.float64)
    qdec = np.repeat(gam[None, :] ** (idx[:, None] + 1.0), DK, axis=1)
    kdec = np.repeat(gam[None, :] ** (CHUNK - 1.0 - idx[:, None]), DK, axis=1)
    rel = idx[:, None] - idx[None, :]
    dm = [np.where(rel >= 0, gam[h] ** np.maximum(rel, 0.0), 0.0) for h in range(HEADS)]
    dmat = np.concatenate(dm, axis=1)
    reps = tile // CHUNK
    return (jnp.asarray(np.tile(qdec, (reps, 1)), F32), jnp.asarray(np.tile(kdec, (reps, 1)), F32),
            jnp.asarray(dmat, F32))


def _layer_weights(l, norm_w, w_in, gla_w2, gla_b, gla_norm_w, ret_norm_w, ret_norm_b, w_out):
    wi = w_in[l]
    lr0 = A_COLS
    wa = wi[:, :lr0].astype(BF16)
    wlr = jnp.pad(wi[:, lr0:lr0 + RANK], ((0, 0), (0, RANK_PAD - RANK))).astype(BF16)
    wb = wi[:, lr0 + RANK:].astype(BF16)
    w2 = jnp.pad(gla_w2[l], ((0, RANK_PAD - RANK), (0, 0))).astype(BF16)
    return (norm_w[l][None, :], wa, wlr, wb, w2, gla_b[l][None, :], gla_norm_w[l][None, :],
            ret_norm_w[l][None, :], ret_norm_b[l][None, :], w_out[l].astype(BF16))


def kernel(x_prompt, x_sample, state_gla, state_ret, norm_w, w_in, gla_w2, gla_b, gla_norm_w,
           ret_norm_w, ret_norm_b, w_out, final_norm_w):
    bp, tp, d = x_prompt.shape
    bs, ts, _ = x_sample.shape
    assert ts == 1, "the decode path handles one new token per sequence"
    depth = w_in.shape[0]
    tile = min(PROMPT_TILE, tp)
    cos_p, sin_p = _rope_tables(np.arange(tp))
    cos_s, sin_s = _rope_tables(PAST_LEN + np.arange(ts))
    ret_tabs = _retention_tables(tile)
    fnw = final_norm_w[None, :]

    hp = x_prompt
    hs = x_sample.reshape(bs, d)
    gla_p, ret_p = [], []
    sample_states = None
    for l in range(depth):
        wts = _layer_weights(l, norm_w, w_in, gla_w2, gla_b, gla_norm_w, ret_norm_w, ret_norm_b,
                             w_out)
        last = l == depth - 1
        hp, sg, sr = _prompt_layer(hp, wts, (cos_p, sin_p) + ret_tabs, fnw, final_norm=last)
        gla_p.append(sg)
        ret_p.append(sr)
        hs, sample_states = _sample_layer(hs, wts, cos_s, sin_s, fnw, state_gla, state_ret,
                                          sample_states, layer=l, final_norm=last)
    return (hp, hs.reshape(bs, ts, d), jnp.stack(gla_p), jnp.stack(ret_p),
            sample_states[0], sample_states[1])
```

```python
import functools

import numpy as np
import jax
import jax.numpy as jnp
from jax import lax
from jax.experimental import pallas as pl
from jax.experimental.pallas import tpu as pltpu

F32 = jnp.float32
BF16 = jnp.bfloat16

HEADS = 4
DK = 64
DV = 128
QK_W = HEADS * DK
V_W = HEADS * DV
RANK = 16
GATE_TAU = 16.0
CHUNK = 64
ROPE_BASE = 10000.0
PAST_LEN = 16384
EPS = 1e-6
GN_EPS = 1e-5
A_COLS = 2 * QK_W + 2 * V_W
LANES = 128
WIN_COLS = 2 * A_COLS + QK_W
FOLD_ROWS = 256
VMEM_LIMIT_BYTES = 56 * 1024 * 1024

PROMPT_TILE = 1024
SAMPLE_BLOCK = 8


def _dot(a, b):
    return jnp.dot(a, b, preferred_element_type=F32)


def _dot_nt(a, b):
    return lax.dot_general(a, b, (((1,), (1,)), ((), ())), preferred_element_type=F32)


def _dot_tn(a, b):
    return lax.dot_general(a, b, (((0,), (0,)), ((), ())), preferred_element_type=F32)


def _rmsnorm(x, w):
    return x * lax.rsqrt(jnp.mean(x * x, axis=-1, keepdims=True) + EPS) * w


def _silu(x):
    return x * (1.0 / (1.0 + jnp.exp(-x)))


def _log_sigmoid(x):
    return jnp.minimum(x, 0.0) - jnp.log1p(jnp.exp(-jnp.abs(x)))


def _ret_gamma(h):
    return 1.0 - 2.0 ** (-5.0 - h)


def _rope(x, cos, sin_signed):
    lane = lax.broadcasted_iota(jnp.int32, (x.shape[0], LANES), 1)
    first_half = (lane % DK) < (DK // 2)
    outs = []
    for j in range(QK_W // LANES):
        xs = x[:, j * LANES:(j + 1) * LANES]
        swapped = jnp.where(first_half,
                            pltpu.roll(xs, LANES - DK // 2, axis=1),
                            pltpu.roll(xs, DK // 2, axis=1))
        outs.append(xs * cos + swapped * sin_signed)
    return jnp.concatenate(outs, axis=1)


def _split_hi_lo(x):
    hi = x.astype(BF16)
    lo = (x - hi.astype(F32)).astype(BF16)
    return hi, lo


def _col_broadcast(row):
    return jnp.broadcast_to(row, (LANES, row.shape[1])).T


def _block_diag_keys(k, head_of_lane):
    return jnp.concatenate(
        [jnp.where(head_of_lane == h, k, jnp.zeros_like(k)) for h in range(HEADS)], axis=0)


def _block_diag_values(blocks):
    z = jnp.zeros_like(blocks[0])
    return jnp.concatenate(
        [jnp.concatenate([blocks[h] if j == h else z for j in range(HEADS)], axis=1)
         for h in range(HEADS)], axis=0)


def _head_blocks(v):
    return [v[:, h * DV:(h + 1) * DV] for h in range(HEADS)]


def _head_norm_gate(oa_ref, ob_ref, ga_ref, gb_ref, gnw_ref, rnw_ref, rnb_ref):
    parts = []
    for j in range(HEADS):
        sl = slice(j * DV, (j + 1) * DV)
        o = oa_ref[:, sl]
        y = o * lax.rsqrt(jnp.mean(o * o, axis=-1, keepdims=True) + EPS) * gnw_ref[...]
        parts.append((y * ga_ref[:, sl]).astype(BF16))
    for j in range(HEADS):
        sl = slice(j * DV, (j + 1) * DV)
        o = ob_ref[:, sl]
        d = o - jnp.mean(o, axis=-1, keepdims=True)
        var = jnp.mean(d * d, axis=-1, keepdims=True)
        y = (d * lax.rsqrt(var + GN_EPS)) * rnw_ref[:, sl] + rnb_ref[:, sl]
        parts.append((y * gb_ref[:, sl]).astype(BF16))
    return jnp.concatenate(parts, axis=1)


def _project(hb, win_ref, gb_ref):
    z = _dot(hb, win_ref[...])
    la = _log_sigmoid(z[:, 0:QK_W] + gb_ref[...]) * (1.0 / GATE_TAU)
    g0, r0 = QK_W, QK_W + A_COLS
    gq = z[:, g0:g0 + QK_W] * (DK ** -0.5)
    gk = z[:, g0 + QK_W:g0 + 2 * QK_W]
    gv = z[:, g0 + 2 * QK_W:g0 + 2 * QK_W + V_W]
    gg = z[:, g0 + 2 * QK_W + V_W:r0]
    rq = z[:, r0:r0 + QK_W]
    rk = z[:, r0 + QK_W:r0 + 2 * QK_W]
    rv = z[:, r0 + 2 * QK_W:r0 + 2 * QK_W + V_W]
    rg = z[:, r0 + 2 * QK_W + V_W:WIN_COLS]
    return gq, gk, gv, gg, la, rq, rk, rv, rg


def _chunk_cumsum(x):
    row_in_chunk = lax.broadcasted_iota(jnp.int32, x.shape, 0) % CHUNK
    shift = 1
    while shift < CHUNK:
        x = x + jnp.where(row_in_chunk >= shift, pltpu.roll(x, shift, axis=0), 0.0)
        shift *= 2
    return x


def _prompt_kernel(x_ref, nw_ref, win_ref, gb_ref, gnw_ref, rnw_ref,
                   rnb_ref, wo_ref, cos_ref, sin_ref, qdec_ref, kdec_ref, dmat_ref, fnw_ref,
                   y_ref, sg_ref, sr_ref,
                   qd_s, kd_s, ke_s, va_s, ga_s, qb_s, qbd_s, kb_s, kbe_s, vb_s, gbt_s,
                   oa_s, ob_s, stg_s, str_s,
                   *, tile, final_norm):
    t = pl.program_id(1)

    @pl.when(t == 0)
    def _init():
        stg_s[...] = jnp.zeros_like(stg_s)
        str_s[...] = jnp.zeros_like(str_s)

    x = x_ref[0]
    hb = _rmsnorm(x, nw_ref[...]).astype(BF16)
    gq, gk, gv, gg, la, rq, rk, rv, rg = _project(hb, win_ref, gb_ref)

    b = _chunk_cumsum(la)
    chunk_decay_rows = []
    b_last = []
    for c in range(tile // CHUNK):
        last = b[(c + 1) * CHUNK - 1:(c + 1) * CHUNK]
        chunk_decay_rows.append(jnp.exp(last))
        b_last.append(jnp.broadcast_to(last, (CHUNK, QK_W)))
    bl = jnp.concatenate(b_last, axis=0)
    qd_s[...] = (gq * jnp.exp(b)).astype(BF16)
    kd_s[...] = (gk * jnp.exp(-b)).astype(BF16)
    ke_s[...] = (gk * jnp.exp(bl - b)).astype(BF16)
    va_s[...] = gv.astype(BF16)
    vb_s[...] = rv.astype(BF16)

    cos = cos_ref[...]
    sin = sin_ref[...]
    qb = _rope(rq, cos, sin)
    kb = _rope(rk, cos, sin) * (DK ** -0.5)
    qb_s[...] = qb.astype(BF16)
    qbd_s[...] = (qb * qdec_ref[...]).astype(BF16)
    kb_s[...] = kb.astype(BF16)
    kbe_s[...] = (kb * kdec_ref[...]).astype(BF16)
    ga_s[...] = _silu(gg)
    gbt_s[...] = _silu(rg)

    dmat = dmat_ref[...]
    tril = dmat > 0.0
    head_of_lane = lax.broadcasted_iota(jnp.int32, (CHUNK, QK_W), 1) // DK
    n_chunks = tile // CHUNK
    chunk_rows = [slice(c * CHUNK, (c + 1) * CHUNK) for c in range(n_chunks)]
    att_gla, att_ret = [], []
    for rows in chunk_rows:
        scores = _dot_nt(qd_s[rows, :], _block_diag_keys(kd_s[rows, :], head_of_lane))
        att_gla.append(jnp.where(tril, scores, 0.0).astype(BF16))
        scores = _dot_nt(qb_s[rows, :], _block_diag_keys(kb_s[rows, :], head_of_lane))
        att_ret.append((scores * dmat).astype(BF16))
    kv_gla, kv_ret = [], []
    for rows in chunk_rows:
        ke, v = ke_s[rows, :], _head_blocks(va_s[rows, :])
        kv_gla.append([_dot_tn(ke[:, h * DK:(h + 1) * DK], v[h]) for h in range(HEADS)])
        kbe, vr = kbe_s[rows, :], _head_blocks(vb_s[rows, :])
        kv_ret.append([_dot_tn(kbe[:, h * DK:(h + 1) * DK], vr[h]) for h in range(HEADS)])
    s_gla = [stg_s[h] for h in range(HEADS)]
    s_ret = [str_s[h] for h in range(HEADS)]
    sin_gla, sin_ret = [], []
    for c in range(n_chunks):
        sin_gla.append([s.astype(BF16) for s in s_gla])
        sin_ret.append([s.astype(BF16) for s in s_ret])
        ecol = _col_broadcast(chunk_decay_rows[c])
        s_gla = [ecol[h * DK:(h + 1) * DK] * s_gla[h] + kv_gla[c][h] for h in range(HEADS)]
        s_ret = [_ret_gamma(h) ** CHUNK * s_ret[h] + kv_ret[c][h] for h in range(HEADS)]
    for h in range(HEADS):
        stg_s[h] = s_gla[h]
        str_s[h] = s_ret[h]
    for c, rows in enumerate(chunk_rows):
        oa_s[rows, :] = (_dot(att_gla[c], _block_diag_values(_head_blocks(va_s[rows, :])))
                         + _dot(qd_s[rows, :], _block_diag_values(sin_gla[c])))
        ob_s[rows, :] = (_dot(att_ret[c], _block_diag_values(_head_blocks(vb_s[rows, :])))
                         + _dot(qbd_s[rows, :], _block_diag_values(sin_ret[c])))

    mixed = _head_norm_gate(oa_s, ob_s, ga_s, gbt_s, gnw_ref, rnw_ref, rnb_ref)
    y = x + _dot(mixed, wo_ref[...])
    if final_norm:
        y = _rmsnorm(y, fnw_ref[...])
    y_ref[0] = y

    @pl.when(t == pl.num_programs(1) - 1)
    def _emit_state():
        sg_ref[0] = stg_s[...]
        sr_ref[0] = str_s[...]


def _const_spec(shape):
    nd = len(shape)
    return pl.BlockSpec(shape, lambda *_: (0,) * nd)


def _layer_spec(shape, layer):
    nd = len(shape)
    return pl.BlockSpec((None,) + tuple(shape[1:]), lambda *_: (layer,) + (0,) * (nd - 1))


def _prompt_layer(x, wts, tables, fnw, *, layer, final_norm):
    bsz, seq, d = x.shape
    tile = min(PROMPT_TILE, seq)
    assert seq % tile == 0
    nt = seq // tile
    cos, sin, qdec, kdec, dmat = tables
    consts = list(wts) + [cos, sin, qdec, kdec, dmat, fnw]
    in_specs = [pl.BlockSpec((1, tile, d), lambda b, t: (b, t, 0))]
    for a in wts:
        in_specs.append(_layer_spec(a.shape, layer))
    in_specs += [pl.BlockSpec((tile, LANES), lambda b, t: (t, 0)),
                 pl.BlockSpec((tile, LANES), lambda b, t: (t, 0)),
                 _const_spec(qdec.shape), _const_spec(kdec.shape), _const_spec(dmat.shape),
                 _const_spec(fnw.shape)]
    state_shape = jax.ShapeDtypeStruct((bsz, HEADS, DK, DV), F32)
    state_spec = pl.BlockSpec((1, HEADS, DK, DV), lambda b, t: (b, 0, 0, 0))
    scratch = [
        pltpu.VMEM((tile, QK_W), BF16),
        pltpu.VMEM((tile, QK_W), BF16),
        pltpu.VMEM((tile, QK_W), BF16),
        pltpu.VMEM((tile, V_W), BF16),
        pltpu.VMEM((tile, V_W), F32),
        pltpu.VMEM((tile, QK_W), BF16),
        pltpu.VMEM((tile, QK_W), BF16),
        pltpu.VMEM((tile, QK_W), BF16),
        pltpu.VMEM((tile, QK_W), BF16),
        pltpu.VMEM((tile, V_W), BF16),
        pltpu.VMEM((tile, V_W), F32),
        pltpu.VMEM((tile, V_W), F32),
        pltpu.VMEM((tile, V_W), F32),
        pltpu.VMEM((HEADS, DK, DV), F32),
        pltpu.VMEM((HEADS, DK, DV), F32),
    ]
    return pl.pallas_call(
        functools.partial(_prompt_kernel, tile=tile, final_norm=final_norm),
        grid=(bsz, nt),
        in_specs=in_specs,
        out_specs=[pl.BlockSpec((1, tile, d), lambda b, t: (b, t, 0)), state_spec, state_spec],
        out_shape=[jax.ShapeDtypeStruct(x.shape, F32), state_shape, state_shape],
        scratch_shapes=scratch,
        compiler_params=pltpu.CompilerParams(
            dimension_semantics=("arbitrary", "arbitrary"),
            vmem_limit_bytes=VMEM_LIMIT_BYTES),
        name="prompt_layer",
    )(x, *consts)


def _sample_kernel(x_ref, nw_ref, win_ref, gb_ref, gnw_ref, rnw_ref,
                   rnb_ref, wo_ref, cos_ref, sin_ref, fnw_ref, sg_in, sr_in, *rest,
                   block, final_norm, chained):
    if chained:
        rest = rest[2:]
    (y_ref, sg_out, sr_out,
     qa_s, ka_s, al_s, va_s, ga_s, qb_s, kb_s, vb_s, gbt_s, oa_s, ob_s) = rest
    i = pl.program_id(0)

    @pl.when(i == 0)
    def _project_all():
        hb = _rmsnorm(x_ref[...], nw_ref[...]).astype(BF16)
        gq, gk, gv, gg, la, rq, rk, rv, rg = _project(hb, win_ref, gb_ref)
        qa_s[...] = gq
        ka_s[...] = gk
        al_s[...] = jnp.exp(la)
        va_s[...] = gv
        ga_s[...] = _silu(gg)
        qb_s[...] = _rope(rq, cos_ref[...], sin_ref[...])
        kb_s[...] = _rope(rk, cos_ref[...], sin_ref[...]) * (DK ** -0.5)
        vb_s[...] = rv
        gbt_s[...] = _silu(rg)

    rows = pl.ds(pl.multiple_of(i * block, block), block)
    qa, ka, al, va = qa_s[rows, :], ka_s[rows, :], al_s[rows, :], va_s[rows, :]
    qb, kb, vb = qb_s[rows, :], kb_s[rows, :], vb_s[rows, :]
    oa_rows, ob_rows = [], []
    for j in range(block):
        one = slice(j, j + 1)
        qcol = _col_broadcast(qa[one])
        kcol = _col_broadcast(ka[one])
        acol = _col_broadcast(al[one])
        qrc = _col_broadcast(qb[one])
        krc = _col_broadcast(kb[one])
        oa_h, ob_h = [], []
        for h in range(HEADS):
            ks = slice(h * DK, (h + 1) * DK)
            vs = slice(h * DV, (h + 1) * DV)
            s_new = acol[ks] * sg_in[j, h] + kcol[ks] * va[one, vs]
            sg_out[j, h] = s_new
            oa_h.append(jnp.sum(qcol[ks] * s_new, axis=0, keepdims=True))
            r_new = _ret_gamma(h) * sr_in[j, h] + krc[ks] * vb[one, vs]
            sr_out[j, h] = r_new
            ob_h.append(jnp.sum(qrc[ks] * r_new, axis=0, keepdims=True))
        oa_rows.append(jnp.concatenate(oa_h, axis=1))
        ob_rows.append(jnp.concatenate(ob_h, axis=1))
    oa_s[rows, :] = jnp.concatenate(oa_rows, axis=0)
    ob_s[rows, :] = jnp.concatenate(ob_rows, axis=0)

    @pl.when(i == pl.num_programs(0) - 1)
    def _finish():
        mixed = _head_norm_gate(oa_s, ob_s, ga_s, gbt_s, gnw_ref, rnw_ref, rnb_ref)
        y = x_ref[...] + _dot(mixed, wo_ref[...])
        if final_norm:
            y = _rmsnorm(y, fnw_ref[...])
        y_ref[...] = y


def _sample_layer(x, wts, cos, sin, fnw, sg, sr, prev_out, *, layer, final_norm):
    n, d = x.shape
    block = SAMPLE_BLOCK
    assert n % block == 0
    consts = list(wts) + [cos, sin, fnw]
    in_specs = ([_const_spec(x.shape)] + [_layer_spec(a.shape, layer) for a in wts]
                + [_const_spec(a.shape) for a in (cos, sin, fnw)])
    st_spec = pl.BlockSpec((None, block, HEADS, DK, DV), lambda i: (layer, i, 0, 0, 0))
    in_specs += [st_spec, st_spec]
    operands = [x, *consts, sg, sr]
    aliases = {}
    if prev_out is not None:
        in_specs += [pl.BlockSpec(memory_space=pl.ANY)] * 2
        aliases = {len(operands): 1, len(operands) + 1: 2}
        operands += list(prev_out)
    scratch = [pltpu.VMEM((n, QK_W), F32)] * 3 + [pltpu.VMEM((n, V_W), F32)] * 2 \
        + [pltpu.VMEM((n, QK_W), F32)] * 2 + [pltpu.VMEM((n, V_W), F32)] * 2 \
        + [pltpu.VMEM((n, V_W), F32)] * 2
    y, sg_new, sr_new = pl.pallas_call(
        functools.partial(_sample_kernel, block=block, final_norm=final_norm,
                          chained=prev_out is not None),
        grid=(n // block,),
        in_specs=in_specs,
        out_specs=[_const_spec(x.shape), st_spec, st_spec],
        out_shape=[jax.ShapeDtypeStruct(x.shape, F32),
                   jax.ShapeDtypeStruct(sg.shape, F32), jax.ShapeDtypeStruct(sr.shape, F32)],
        input_output_aliases=aliases,
        scratch_shapes=scratch,
        compiler_params=pltpu.CompilerParams(
            dimension_semantics=("arbitrary",), vmem_limit_bytes=VMEM_LIMIT_BYTES),
        name="sample_layer",
    )(*operands)
    return y, (sg_new, sr_new)


def _rope_tables(pos):
    half = DK // 2
    inv = ROPE_BASE ** (-np.arange(half, dtype=np.float64) / half)
    ang = np.asarray(pos, dtype=np.float64)[:, None] * inv[None, :]
    cos = np.tile(np.cos(ang), (1, LANES // half))
    sin = np.tile(np.concatenate([-np.sin(ang), np.sin(ang)], axis=1), (1, LANES // DK))
    return jnp.asarray(cos, F32), jnp.asarray(sin, F32)


def _retention_tables(tile):
    idx = np.arange(CHUNK, dtype=np.float64)
    gam = np.array([_ret_gamma(h) for h in range(HEADS)], dtype=np.float64)
    qdec = np.repeat(gam[None, :] ** (idx[:, None] + 1.0), DK, axis=1)
    kdec = np.repeat(gam[None, :] ** (CHUNK - 1.0 - idx[:, None]), DK, axis=1)
    rel = idx[:, None] - idx[None, :]
    dm = [np.where(rel >= 0, gam[h] ** np.maximum(rel, 0.0), 0.0) for h in range(HEADS)]
    dmat = np.concatenate(dm, axis=1)
    reps = tile // CHUNK
    return (jnp.asarray(np.tile(qdec, (reps, 1)), F32), jnp.asarray(np.tile(kdec, (reps, 1)), F32),
            jnp.asarray(dmat, F32))


def _fold_weights_kernel(win_ref, w2_ref, wout_ref, fused_ref, wout_bf_ref):
    w = win_ref[...]
    a_hi, a_lo = _split_hi_lo(w[:, A_COLS:A_COLS + RANK])
    b_hi, b_lo = _split_hi_lo(w2_ref[...])
    folded = _dot(a_hi, b_hi) + _dot(a_hi, b_lo) + _dot(a_lo, b_hi)
    fused_ref[:, 0:QK_W] = folded.astype(BF16)
    fused_ref[:, QK_W:QK_W + A_COLS] = w[:, 0:A_COLS].astype(BF16)
    fused_ref[:, QK_W + A_COLS:WIN_COLS] = w[:, A_COLS + RANK:2 * A_COLS + RANK].astype(BF16)
    wout_bf_ref[...] = wout_ref[...].astype(BF16)


def _fold_weights(w_in, gla_w2, w_out):
    depth, d, in_cols = w_in.shape
    assert in_cols == 2 * A_COLS + RANK and d % FOLD_ROWS == 0
    return pl.pallas_call(
        _fold_weights_kernel,
        grid=(depth, d // FOLD_ROWS),
        in_specs=[pl.BlockSpec((None, FOLD_ROWS, in_cols), lambda l, i: (l, i, 0)),
                  pl.BlockSpec((None, RANK, QK_W), lambda l, i: (l, 0, 0)),
                  pl.BlockSpec((None, FOLD_ROWS, w_out.shape[2]), lambda l, i: (l, i, 0))],
        out_specs=[pl.BlockSpec((None, FOLD_ROWS, WIN_COLS), lambda l, i: (l, i, 0)),
                   pl.BlockSpec((None, FOLD_ROWS, w_out.shape[2]), lambda l, i: (l, i, 0))],
        out_shape=[jax.ShapeDtypeStruct((depth, d, WIN_COLS), BF16),
                   jax.ShapeDtypeStruct(w_out.shape, BF16)],
        compiler_params=pltpu.CompilerParams(dimension_semantics=("arbitrary", "arbitrary")),
        name="fold_weights",
    )(w_in, gla_w2, w_out)


def _stacked_params(fused_in, wout_bf, norm_w, gla_b, gla_norm_w, ret_norm_w, ret_norm_b):
    row = lambda p: p[:, None, :]
    return (row(norm_w), fused_in, row(gla_b), row(gla_norm_w), row(ret_norm_w), row(ret_norm_b),
            wout_bf)


def kernel(x_prompt, x_sample, state_gla, state_ret, norm_w, w_in, gla_w2, gla_b, gla_norm_w,
           ret_norm_w, ret_norm_b, w_out, final_norm_w):
    bp, tp, d = x_prompt.shape
    bs, ts, _ = x_sample.shape
    assert ts == 1, "the decode path handles one new token per sequence"
    depth = w_in.shape[0]
    tile = min(PROMPT_TILE, tp)
    cos_p, sin_p = _rope_tables(np.arange(tp))
    cos_s, sin_s = _rope_tables(PAST_LEN + np.arange(ts))
    ret_tabs = _retention_tables(tile)
    fnw = final_norm_w[None, :]

    hp = x_prompt
    hs = x_sample.reshape(bs, d)
    gla_p, ret_p = [], []
    sample_states = None
    fused_in, wout_bf = _fold_weights(w_in, gla_w2, w_out)
    wts = _stacked_params(fused_in, wout_bf, norm_w, gla_b, gla_norm_w, ret_norm_w, ret_norm_b)
    for l in range(depth):
        last = l == depth - 1
        hp, sg, sr = _prompt_layer(hp, wts, (cos_p, sin_p) + ret_tabs, fnw, layer=l,
                                   final_norm=last)
        gla_p.append(sg)
        ret_p.append(sr)
        hs, sample_states = _sample_layer(hs, wts, cos_s, sin_s, fnw, state_gla, state_ret,
                                          sample_states, layer=l, final_norm=last)
    return (hp, hs.reshape(bs, ts, d), jnp.stack(gla_p), jnp.stack(ret_p),
            sample_states[0], sample_states[1])
```

```python
import functools

import numpy as np
import jax
import jax.numpy as jnp
from jax import lax
from jax.experimental import pallas as pl
from jax.experimental.pallas import tpu as pltpu

F32 = jnp.float32
BF16 = jnp.bfloat16

HEADS = 4
DK = 64
DV = 128
QK_W = HEADS * DK
V_W = HEADS * DV
RANK = 16
GATE_TAU = 16.0
CHUNK = 64
ROPE_BASE = 10000.0
PAST_LEN = 16384
EPS = 1e-6
GN_EPS = 1e-5
A_COLS = 2 * QK_W + 2 * V_W
LANES = 128
WIN_COLS = 2 * A_COLS + QK_W
FOLD_COLS = 256
SUBLANES = 8
VMEM_LIMIT_BYTES = 56 * 1024 * 1024

PROMPT_TILE = 1024


def _dot(a, b):
    return jnp.dot(a, b, preferred_element_type=F32)


def _dot_nt(a, b):
    return lax.dot_general(a, b, (((1,), (1,)), ((), ())), preferred_element_type=F32)


def _dot_tn(a, b):
    return lax.dot_general(a, b, (((0,), (0,)), ((), ())), preferred_element_type=F32)


def _rmsnorm(x, w):
    return x * lax.rsqrt(jnp.mean(x * x, axis=-1, keepdims=True) + EPS) * w


def _silu(x):
    return x * (1.0 / (1.0 + jnp.exp(-x)))


def _log_sigmoid(x):
    return jnp.minimum(x, 0.0) - jnp.log1p(jnp.exp(-jnp.abs(x)))


def _ret_gamma(h):
    return 1.0 - 2.0 ** (-5.0 - h)


def _rope(x, cos, sin_signed):
    lane = lax.broadcasted_iota(jnp.int32, (x.shape[0], LANES), 1)
    first_half = (lane % DK) < (DK // 2)
    outs = []
    for j in range(QK_W // LANES):
        xs = x[:, j * LANES:(j + 1) * LANES]
        swapped = jnp.where(first_half,
                            pltpu.roll(xs, LANES - DK // 2, axis=1),
                            pltpu.roll(xs, DK // 2, axis=1))
        outs.append(xs * cos + swapped * sin_signed)
    return jnp.concatenate(outs, axis=1)


def _split_hi_lo(x):
    hi = x.astype(BF16)
    lo = (x - hi.astype(F32)).astype(BF16)
    return hi, lo


def _col_broadcast(row):
    return jnp.broadcast_to(row, (LANES, row.shape[1])).T


def _block_diag_keys(k, head_of_lane):
    return jnp.concatenate(
        [jnp.where(head_of_lane == h, k, jnp.zeros_like(k)) for h in range(HEADS)], axis=0)


def _block_diag_values(blocks):
    z = jnp.zeros_like(blocks[0])
    return jnp.concatenate(
        [jnp.concatenate([blocks[h] if j == h else z for j in range(HEADS)], axis=1)
         for h in range(HEADS)], axis=0)


def _head_blocks(v):
    return [v[:, h * DV:(h + 1) * DV] for h in range(HEADS)]


def _head_norm_gate(oa_ref, ob_ref, ga_ref, gb_ref, gnw_ref, rnw_ref, rnb_ref):
    parts = []
    for j in range(HEADS):
        sl = slice(j * DV, (j + 1) * DV)
        o = oa_ref[:, sl]
        y = o * lax.rsqrt(jnp.mean(o * o, axis=-1, keepdims=True) + EPS) * gnw_ref[...]
        parts.append((y * ga_ref[:, sl]).astype(BF16))
    for j in range(HEADS):
        sl = slice(j * DV, (j + 1) * DV)
        o = ob_ref[:, sl]
        d = o - jnp.mean(o, axis=-1, keepdims=True)
        var = jnp.mean(d * d, axis=-1, keepdims=True)
        y = (d * lax.rsqrt(var + GN_EPS)) * rnw_ref[:, sl] + rnb_ref[:, sl]
        parts.append((y * gb_ref[:, sl]).astype(BF16))
    return jnp.concatenate(parts, axis=1)


def _project(hb, win_ref, gb_ref):
    z = _dot(hb, win_ref[...])
    la = _log_sigmoid(z[:, 0:QK_W] + gb_ref[...]) * (1.0 / GATE_TAU)
    g0, r0 = QK_W, QK_W + A_COLS
    gq = z[:, g0:g0 + QK_W] * (DK ** -0.5)
    gk = z[:, g0 + QK_W:g0 + 2 * QK_W]
    gv = z[:, g0 + 2 * QK_W:g0 + 2 * QK_W + V_W]
    gg = z[:, g0 + 2 * QK_W + V_W:r0]
    rq = z[:, r0:r0 + QK_W]
    rk = z[:, r0 + QK_W:r0 + 2 * QK_W]
    rv = z[:, r0 + 2 * QK_W:r0 + 2 * QK_W + V_W]
    rg = z[:, r0 + 2 * QK_W + V_W:WIN_COLS]
    return gq, gk, gv, gg, la, rq, rk, rv, rg


def _chunk_cumsum(x):
    row_in_chunk = lax.broadcasted_iota(jnp.int32, x.shape, 0) % CHUNK
    shift = 1
    while shift < CHUNK:
        x = x + jnp.where(row_in_chunk >= shift, pltpu.roll(x, shift, axis=0), 0.0)
        shift *= 2
    return x


N_PROMPT_SCRATCH = 15
N_SAMPLE_SCRATCH = 11


def _prompt_tile(x_ref, nw_ref, win_ref, gb_ref, gnw_ref, rnw_ref, rnb_ref, wo_ref,
                 cos_ref, sin_ref, qdec_ref, kdec_ref, dmat_ref, fnw_ref, y_ref,
                 qd_s, kd_s, ke_s, va_s, ga_s, qb_s, qbd_s, kb_s, kbe_s, vb_s, gbt_s,
                 oa_s, ob_s, stg_s, str_s, *, tile, final_norm):
    x = x_ref[0]
    hb = _rmsnorm(x, nw_ref[...]).astype(BF16)
    gq, gk, gv, gg, la, rq, rk, rv, rg = _project(hb, win_ref, gb_ref)

    b = _chunk_cumsum(la)
    chunk_decay_rows = []
    b_last = []
    for c in range(tile // CHUNK):
        last = b[(c + 1) * CHUNK - 1:(c + 1) * CHUNK]
        chunk_decay_rows.append(jnp.exp(last))
        b_last.append(jnp.broadcast_to(last, (CHUNK, QK_W)))
    bl = jnp.concatenate(b_last, axis=0)
    qd_s[...] = (gq * jnp.exp(b)).astype(BF16)
    kd_s[...] = (gk * jnp.exp(-b)).astype(BF16)
    ke_s[...] = (gk * jnp.exp(bl - b)).astype(BF16)
    va_s[...] = gv.astype(BF16)
    vb_s[...] = rv.astype(BF16)

    cos = cos_ref[...]
    sin = sin_ref[...]
    qb = _rope(rq, cos, sin)
    kb = _rope(rk, cos, sin) * (DK ** -0.5)
    qb_s[...] = qb.astype(BF16)
    qbd_s[...] = (qb * qdec_ref[...]).astype(BF16)
    kb_s[...] = kb.astype(BF16)
    kbe_s[...] = (kb * kdec_ref[...]).astype(BF16)
    ga_s[...] = _silu(gg)
    gbt_s[...] = _silu(rg)

    dmat = dmat_ref[...]
    tril = dmat > 0.0
    head_of_lane = lax.broadcasted_iota(jnp.int32, (CHUNK, QK_W), 1) // DK
    n_chunks = tile // CHUNK
    chunk_rows = [slice(c * CHUNK, (c + 1) * CHUNK) for c in range(n_chunks)]
    att_gla, att_ret = [], []
    for rows in chunk_rows:
        scores = _dot_nt(qd_s[rows, :], _block_diag_keys(kd_s[rows, :], head_of_lane))
        att_gla.append(jnp.where(tril, scores, 0.0).astype(BF16))
        scores = _dot_nt(qb_s[rows, :], _block_diag_keys(kb_s[rows, :], head_of_lane))
        att_ret.append((scores * dmat).astype(BF16))
    kv_gla, kv_ret = [], []
    for rows in chunk_rows:
        ke, v = ke_s[rows, :], _head_blocks(va_s[rows, :])
        kv_gla.append([_dot_tn(ke[:, h * DK:(h + 1) * DK], v[h]) for h in range(HEADS)])
        kbe, vr = kbe_s[rows, :], _head_blocks(vb_s[rows, :])
        kv_ret.append([_dot_tn(kbe[:, h * DK:(h + 1) * DK], vr[h]) for h in range(HEADS)])
    s_gla = [stg_s[h] for h in range(HEADS)]
    s_ret = [str_s[h] for h in range(HEADS)]
    sin_gla, sin_ret = [], []
    for c in range(n_chunks):
        sin_gla.append([s.astype(BF16) for s in s_gla])
        sin_ret.append([s.astype(BF16) for s in s_ret])
        ecol = _col_broadcast(chunk_decay_rows[c])
        s_gla = [ecol[h * DK:(h + 1) * DK] * s_gla[h] + kv_gla[c][h] for h in range(HEADS)]
        s_ret = [_ret_gamma(h) ** CHUNK * s_ret[h] + kv_ret[c][h] for h in range(HEADS)]
    for h in range(HEADS):
        stg_s[h] = s_gla[h]
        str_s[h] = s_ret[h]
    for c, rows in enumerate(chunk_rows):
        oa_s[rows, :] = (_dot(att_gla[c], _block_diag_values(_head_blocks(va_s[rows, :])))
                         + _dot(qd_s[rows, :], _block_diag_values(sin_gla[c])))
        ob_s[rows, :] = (_dot(att_ret[c], _block_diag_values(_head_blocks(vb_s[rows, :])))
                         + _dot(qbd_s[rows, :], _block_diag_values(sin_ret[c])))

    mixed = _head_norm_gate(oa_s, ob_s, ga_s, gbt_s, gnw_ref, rnw_ref, rnb_ref)
    y = x + _dot(mixed, wo_ref[...])
    if final_norm:
        y = _rmsnorm(y, fnw_ref[...])
    y_ref[0] = y


def _sample_project(x_ref, nw_ref, win_ref, gb_ref, cos_ref, sin_ref,
                    qa_s, ka_s, al_s, va_s, ga_s, qb_s, kb_s, vb_s, gbt_s):
    hb = _rmsnorm(x_ref[...], nw_ref[...]).astype(BF16)
    gq, gk, gv, gg, la, rq, rk, rv, rg = _project(hb, win_ref, gb_ref)
    qa_s[...] = gq
    ka_s[...] = gk
    al_s[...] = jnp.exp(la)
    va_s[...] = gv
    ga_s[...] = _silu(gg)
    qb_s[...] = _rope(rq, cos_ref[...], sin_ref[...])
    kb_s[...] = _rope(rk, cos_ref[...], sin_ref[...]) * (DK ** -0.5)
    vb_s[...] = rv
    gbt_s[...] = _silu(rg)


def _sample_update(step, block, sg_in, sr_in, sg_out, sr_out,
                   qa_s, ka_s, al_s, va_s, qb_s, kb_s, vb_s, oa_s, ob_s):
    rows = pl.ds(pl.multiple_of(step * block, block), block)
    qa, ka, al, va = qa_s[rows, :], ka_s[rows, :], al_s[rows, :], va_s[rows, :]
    qb, kb, vb = qb_s[rows, :], kb_s[rows, :], vb_s[rows, :]
    oa_rows, ob_rows = [], []
    for j in range(block):
        one = slice(j, j + 1)
        qcol = _col_broadcast(qa[one])
        kcol = _col_broadcast(ka[one])
        acol = _col_broadcast(al[one])
        qrc = _col_broadcast(qb[one])
        krc = _col_broadcast(kb[one])
        oa_h, ob_h = [], []
        for h in range(HEADS):
            ks = slice(h * DK, (h + 1) * DK)
            vs = slice(h * DV, (h + 1) * DV)
            s_new = acol[ks] * sg_in[j, h] + kcol[ks] * va[one, vs]
            sg_out[j, h] = s_new
            oa_h.append(jnp.sum(qcol[ks] * s_new, axis=0, keepdims=True))
            r_new = _ret_gamma(h) * sr_in[j, h] + krc[ks] * vb[one, vs]
            sr_out[j, h] = r_new
            ob_h.append(jnp.sum(qrc[ks] * r_new, axis=0, keepdims=True))
        oa_rows.append(jnp.concatenate(oa_h, axis=1))
        ob_rows.append(jnp.concatenate(ob_h, axis=1))
    oa_s[rows, :] = jnp.concatenate(oa_rows, axis=0)
    ob_s[rows, :] = jnp.concatenate(ob_rows, axis=0)


def _layer_kernel(*refs, tile, block, final_norm, chained):
    (x_ref, nw_ref, win_ref, gb_ref, gnw_ref, rnw_ref, rnb_ref, wo_ref,
     cos_ref, sin_ref, qdec_ref, kdec_ref, dmat_ref, fnw_ref,
     xs_ref, cos_s_ref, sin_s_ref, sg_in, sr_in) = refs[:19]
    refs = refs[19 + (2 if chained else 0):]
    y_ref, sg_ref, sr_ref, ys_ref, sg_out, sr_out = refs[:6]
    prompt_scratch = refs[6:6 + N_PROMPT_SCRATCH]
    (qa_s, ka_s, al_s, vsa_s, gsa_s, qsb_s, ksb_s, vsb_s, gsb_s, osa_s, osb_s) = \
        refs[6 + N_PROMPT_SCRATCH:]
    stg_s, str_s = prompt_scratch[-2:]
    t = pl.program_id(1)
    step = pl.program_id(0) * pl.num_programs(1) + t
    last_step = pl.num_programs(0) * pl.num_programs(1) - 1

    @pl.when(t == 0)
    def _zero_prompt_states():
        stg_s[...] = jnp.zeros_like(stg_s)
        str_s[...] = jnp.zeros_like(str_s)

    @pl.when(step == 0)
    def _project_decode_rows():
        _sample_project(xs_ref, nw_ref, win_ref, gb_ref, cos_s_ref, sin_s_ref,
                        qa_s, ka_s, al_s, vsa_s, gsa_s, qsb_s, ksb_s, vsb_s, gsb_s)

    _sample_update(step, block, sg_in, sr_in, sg_out, sr_out,
                   qa_s, ka_s, al_s, vsa_s, qsb_s, ksb_s, vsb_s, osa_s, osb_s)
    _prompt_tile(x_ref, nw_ref, win_ref, gb_ref, gnw_ref, rnw_ref, rnb_ref, wo_ref,
                 cos_ref, sin_ref, qdec_ref, kdec_ref, dmat_ref, fnw_ref, y_ref,
                 *prompt_scratch, tile=tile, final_norm=final_norm)

    @pl.when(t == pl.num_programs(1) - 1)
    def _emit_prompt_states():
        sg_ref[0] = stg_s[...]
        sr_ref[0] = str_s[...]

    @pl.when(step == last_step)
    def _finish_decode_rows():
        mixed = _head_norm_gate(osa_s, osb_s, gsa_s, gsb_s, gnw_ref, rnw_ref, rnb_ref)
        ys = xs_ref[...] + _dot(mixed, wo_ref[...])
        if final_norm:
            ys = _rmsnorm(ys, fnw_ref[...])
        ys_ref[...] = ys


def _const_spec(shape):
    nd = len(shape)
    return pl.BlockSpec(shape, lambda *_: (0,) * nd)


def _layer_spec(shape, layer):
    nd = len(shape)
    return pl.BlockSpec((None,) + tuple(shape[1:]), lambda *_: (layer,) + (0,) * (nd - 1))


def _layer(x, xs, wts, tables, sample_tables, fnw, sg, sr, prev_out, *, layer, final_norm):
    bsz, seq, d = x.shape
    n = xs.shape[0]
    tile = min(PROMPT_TILE, seq)
    assert seq % tile == 0
    nt = seq // tile
    steps = bsz * nt
    block = n // steps
    assert block * steps == n and block % SUBLANES == 0, "decode rows must split evenly over steps"
    cos, sin, qdec, kdec, dmat = tables
    cos_s, sin_s = sample_tables
    tile_spec = pl.BlockSpec((1, tile, d), lambda b, t: (b, t, 0))
    in_specs = [tile_spec] + [_layer_spec(a.shape, layer) for a in wts]
    in_specs += [pl.BlockSpec((tile, LANES), lambda b, t: (t, 0)),
                 pl.BlockSpec((tile, LANES), lambda b, t: (t, 0)),
                 _const_spec(qdec.shape), _const_spec(kdec.shape), _const_spec(dmat.shape),
                 _const_spec(fnw.shape), _const_spec(xs.shape), _const_spec(cos_s.shape),
                 _const_spec(sin_s.shape)]
    st_spec = pl.BlockSpec((None, block, HEADS, DK, DV), lambda b, t: (layer, b * nt + t, 0, 0, 0))
    in_specs += [st_spec, st_spec]
    operands = [x, *wts, cos, sin, qdec, kdec, dmat, fnw, xs, cos_s, sin_s, sg, sr]
    aliases = {}
    if prev_out is not None:
        in_specs += [pl.BlockSpec(memory_space=pl.ANY)] * 2
        aliases = {len(operands): 4, len(operands) + 1: 5}
        operands += list(prev_out)
    pstate_shape = jax.ShapeDtypeStruct((bsz, HEADS, DK, DV), F32)
    pstate_spec = pl.BlockSpec((1, HEADS, DK, DV), lambda b, t: (b, 0, 0, 0))
    prompt_scratch = (
        [pltpu.VMEM((tile, QK_W), BF16)] * 3
        + [pltpu.VMEM((tile, V_W), BF16)]
        + [pltpu.VMEM((tile, V_W), F32)]
        + [pltpu.VMEM((tile, QK_W), BF16)] * 4
        + [pltpu.VMEM((tile, V_W), BF16)]
        + [pltpu.VMEM((tile, V_W), F32)] * 3
        + [pltpu.VMEM((HEADS, DK, DV), F32)] * 2
    )
    sample_scratch = (
        [pltpu.VMEM((n, QK_W), F32)] * 3
        + [pltpu.VMEM((n, V_W), F32)] * 2
        + [pltpu.VMEM((n, QK_W), F32)] * 2
        + [pltpu.VMEM((n, V_W), F32)] * 4
    )
    assert len(prompt_scratch) == N_PROMPT_SCRATCH and len(sample_scratch) == N_SAMPLE_SCRATCH
    y, sgp, srp, ys, sg_new, sr_new = pl.pallas_call(
        functools.partial(_layer_kernel, tile=tile, block=block, final_norm=final_norm,
                          chained=prev_out is not None),
        grid=(bsz, nt),
        in_specs=in_specs,
        out_specs=[tile_spec, pstate_spec, pstate_spec, _const_spec(xs.shape), st_spec, st_spec],
        out_shape=[jax.ShapeDtypeStruct(x.shape, F32), pstate_shape, pstate_shape,
                   jax.ShapeDtypeStruct(xs.shape, F32),
                   jax.ShapeDtypeStruct(sg.shape, F32), jax.ShapeDtypeStruct(sr.shape, F32)],
        input_output_aliases=aliases,
        scratch_shapes=prompt_scratch + sample_scratch,
        compiler_params=pltpu.CompilerParams(
            dimension_semantics=("arbitrary", "arbitrary"),
            vmem_limit_bytes=VMEM_LIMIT_BYTES),
        name="hybrid_layer",
    )(*operands)
    return y, sgp, srp, ys, (sg_new, sr_new)


def _rope_tables(pos):
    half = DK // 2
    inv = ROPE_BASE ** (-np.arange(half, dtype=np.float64) / half)
    ang = np.asarray(pos, dtype=np.float64)[:, None] * inv[None, :]
    cos = np.tile(np.cos(ang), (1, LANES // half))
    sin = np.tile(np.concatenate([-np.sin(ang), np.sin(ang)], axis=1), (1, LANES // DK))
    return jnp.asarray(cos, F32), jnp.asarray(sin, F32)


def _retention_tables(tile):
    idx = np.arange(CHUNK, dtype=np.float64)
    gam = np.array([_ret_gamma(h) for h in range(HEADS)], dtype=np.float64)
    qdec = np.repeat(gam[None, :] ** (idx[:, None] + 1.0), DK, axis=1)
    kdec = np.repeat(gam[None, :] ** (CHUNK - 1.0 - idx[:, None]), DK, axis=1)
    rel = idx[:, None] - idx[None, :]
    dm = [np.where(rel >= 0, gam[h] ** np.maximum(rel, 0.0), 0.0) for h in range(HEADS)]
    dmat = np.concatenate(dm, axis=1)
    reps = tile // CHUNK
    return (jnp.asarray(np.tile(qdec, (reps, 1)), F32), jnp.asarray(np.tile(kdec, (reps, 1)), F32),
            jnp.asarray(dmat, F32))


def _fold_weights_kernel(wt_ref, w2_ref, wout_ref, fused_ref, wout_bf_ref):
    j = pl.program_id(1)

    @pl.when(j == 0)
    def _fold_gate():
        a_hi, a_lo = _split_hi_lo(wt_ref[A_COLS:A_COLS + RANK, :])
        b_hi, b_lo = _split_hi_lo(w2_ref[...])
        folded = _dot_tn(a_hi, b_hi) + _dot_tn(a_hi, b_lo) + _dot_tn(a_lo, b_hi)
        fused_ref[...] = folded.astype(BF16)
        wout_bf_ref[...] = wout_ref[...].astype(BF16)

    @pl.when(j > 0)
    def _transpose_block():
        gla_blocks = A_COLS // FOLD_COLS
        start = jnp.where(j <= gla_blocks, (j - 1) * FOLD_COLS, RANK + (j - 1) * FOLD_COLS)
        rows = pl.ds(pl.multiple_of(start, SUBLANES), FOLD_COLS)
        fused_ref[...] = wt_ref[rows, :].T.astype(BF16)


def _fold_weights(w_in, gla_w2, w_out):
    depth, d, in_cols = w_in.shape
    assert in_cols == 2 * A_COLS + RANK and A_COLS % FOLD_COLS == 0 and QK_W == FOLD_COLS
    w_in_t = jnp.swapaxes(w_in, 1, 2)
    return pl.pallas_call(
        _fold_weights_kernel,
        grid=(depth, WIN_COLS // FOLD_COLS),
        in_specs=[pl.BlockSpec((None, in_cols, d), lambda l, j: (l, 0, 0)),
                  pl.BlockSpec((None, RANK, QK_W), lambda l, j: (l, 0, 0)),
                  pl.BlockSpec((None,) + w_out.shape[1:], lambda l, j: (l, 0, 0))],
        out_specs=[pl.BlockSpec((None, d, FOLD_COLS), lambda l, j: (l, 0, j)),
                   pl.BlockSpec((None,) + w_out.shape[1:], lambda l, j: (l, 0, 0))],
        out_shape=[jax.ShapeDtypeStruct((depth, d, WIN_COLS), BF16),
                   jax.ShapeDtypeStruct(w_out.shape, BF16)],
        compiler_params=pltpu.CompilerParams(
            dimension_semantics=("arbitrary", "arbitrary"), vmem_limit_bytes=VMEM_LIMIT_BYTES),
        name="fold_weights",
    )(w_in_t, gla_w2, w_out)


def _stacked_params(fused_in, wout_bf, norm_w, gla_b, gla_norm_w, ret_norm_w, ret_norm_b):
    row = lambda p: p[:, None, :]
    return (row(norm_w), fused_in, row(gla_b), row(gla_norm_w), row(ret_norm_w), row(ret_norm_b),
            wout_bf)


def kernel(x_prompt, x_sample, state_gla, state_ret, norm_w, w_in, gla_w2, gla_b, gla_norm_w,
           ret_norm_w, ret_norm_b, w_out, final_norm_w):
    bp, tp, d = x_prompt.shape
    bs, ts, _ = x_sample.shape
    assert ts == 1, "the decode path handles one new token per sequence"
    depth = w_in.shape[0]
    tile = min(PROMPT_TILE, tp)
    prompt_tables = _rope_tables(np.arange(tp)) + _retention_tables(tile)
    sample_tables = _rope_tables(PAST_LEN + np.arange(ts))
    fnw = final_norm_w[None, :]

    hp = x_prompt
    hs = x_sample.reshape(bs, d)
    gla_p, ret_p = [], []
    sample_states = None
    fused_in, wout_bf = _fold_weights(w_in, gla_w2, w_out)
    wts = _stacked_params(fused_in, wout_bf, norm_w, gla_b, gla_norm_w, ret_norm_w, ret_norm_b)
    for l in range(depth):
        hp, sg, sr, hs, sample_states = _layer(
            hp, hs, wts, prompt_tables, sample_tables, fnw, state_gla, state_ret, sample_states,
            layer=l, final_norm=l == depth - 1)
        gla_p.append(sg)
        ret_p.append(sr)
    return (hp, hs.reshape(bs, ts, d), jnp.stack(gla_p), jnp.stack(ret_p),
            sample_states[0], sample_states[1])
```

```python
import functools

import numpy as np
import jax
import jax.numpy as jnp
from jax import lax
from jax.experimental import pallas as pl
from jax.experimental.pallas import tpu as pltpu

F32 = jnp.float32
BF16 = jnp.bfloat16

HEADS = 4
DK = 64
DV = 128
QK_W = HEADS * DK
V_W = HEADS * DV
RANK = 16
GATE_TAU = 16.0
CHUNK = 64
ROPE_BASE = 10000.0
PAST_LEN = 16384
EPS = 1e-6
GN_EPS = 1e-5
A_COLS = 2 * QK_W + 2 * V_W
LANES = 128
WIN_COLS = 2 * A_COLS + QK_W
FOLD_COLS = 256
FOLD_ROWS = 256
SUBLANES = 8
VMEM_LIMIT_BYTES = 56 * 1024 * 1024

PROMPT_TILE = 1024
PROJ_ROWS = 256


def _dot(a, b):
    return jnp.dot(a, b, preferred_element_type=F32)


def _dot_nt(a, b):
    return lax.dot_general(a, b, (((1,), (1,)), ((), ())), preferred_element_type=F32)


def _dot_tn(a, b):
    return lax.dot_general(a, b, (((0,), (0,)), ((), ())), preferred_element_type=F32)


def _rmsnorm(x, w):
    return x * lax.rsqrt(jnp.mean(x * x, axis=-1, keepdims=True) + EPS) * w


def _silu(x):
    return x * (1.0 / (1.0 + jnp.exp(-x)))


def _log_sigmoid(x):
    return jnp.minimum(x, 0.0) - jnp.log1p(jnp.exp(-jnp.abs(x)))


def _ret_gamma(h):
    return 1.0 - 2.0 ** (-5.0 - h)


def _rope(x, cos, sin_signed):
    lane = lax.broadcasted_iota(jnp.int32, (x.shape[0], LANES), 1)
    first_half = (lane % DK) < (DK // 2)
    outs = []
    for j in range(QK_W // LANES):
        xs = x[:, j * LANES:(j + 1) * LANES]
        swapped = jnp.where(first_half,
                            pltpu.roll(xs, LANES - DK // 2, axis=1),
                            pltpu.roll(xs, DK // 2, axis=1))
        outs.append(xs * cos + swapped * sin_signed)
    return jnp.concatenate(outs, axis=1)


def _split_hi_lo(x):
    hi = x.astype(BF16)
    lo = (x - hi.astype(F32)).astype(BF16)
    return hi, lo


def _col_broadcast(row):
    return jnp.broadcast_to(row, (LANES, row.shape[1])).T


def _block_diag_keys(k, head_of_lane):
    return jnp.concatenate(
        [jnp.where(head_of_lane == h, k, jnp.zeros_like(k)) for h in range(HEADS)], axis=0)


def _block_diag_values(blocks):
    z = jnp.zeros_like(blocks[0])
    return jnp.concatenate(
        [jnp.concatenate([blocks[h] if j == h else z for j in range(HEADS)], axis=1)
         for h in range(HEADS)], axis=0)


def _head_blocks(v):
    return [v[:, h * DV:(h + 1) * DV] for h in range(HEADS)]


def _head_norm_gate(rows, oa_ref, ob_ref, ga_ref, gb_ref, gnw_ref, rnw_ref, rnb_ref):
    parts = []
    for j in range(HEADS):
        sl = slice(j * DV, (j + 1) * DV)
        o = oa_ref[rows, sl]
        y = o * lax.rsqrt(jnp.mean(o * o, axis=-1, keepdims=True) + EPS) * gnw_ref[...]
        parts.append((y * ga_ref[rows, sl]).astype(BF16))
    for j in range(HEADS):
        sl = slice(j * DV, (j + 1) * DV)
        o = ob_ref[rows, sl]
        d = o - jnp.mean(o, axis=-1, keepdims=True)
        var = jnp.mean(d * d, axis=-1, keepdims=True)
        y = (d * lax.rsqrt(var + GN_EPS)) * rnw_ref[:, sl] + rnb_ref[:, sl]
        parts.append((y * gb_ref[rows, sl]).astype(BF16))
    return jnp.concatenate(parts, axis=1)


def _project(x, nw_ref, win_ref, gb_ref):
    pres, zs = [], []
    for r in range(0, x.shape[0], PROJ_ROWS):
        hb = _rmsnorm(x[r:r + PROJ_ROWS], nw_ref[...]).astype(BF16)
        pres.append(_dot(hb, win_ref[:, 0:QK_W]))
        zs.append(_dot(hb, win_ref[:, QK_W:]))
    pre = jnp.concatenate(pres, axis=0)
    z = jnp.concatenate(zs, axis=0)
    la = _log_sigmoid(pre + gb_ref[...]) * (1.0 / GATE_TAU)
    g0, r0 = 0, A_COLS
    gq = z[:, g0:g0 + QK_W] * (DK ** -0.5)
    gk = z[:, g0 + QK_W:g0 + 2 * QK_W]
    gv = z[:, g0 + 2 * QK_W:g0 + 2 * QK_W + V_W]
    gg = z[:, g0 + 2 * QK_W + V_W:r0]
    rq = z[:, r0:r0 + QK_W]
    rk = z[:, r0 + QK_W:r0 + 2 * QK_W]
    rv = z[:, r0 + 2 * QK_W:r0 + 2 * QK_W + V_W]
    rg = z[:, r0 + 2 * QK_W + V_W:2 * A_COLS]
    return gq, gk, gv, gg, la, rq, rk, rv, rg


def _chunk_cumsum(x):
    row_in_chunk = lax.broadcasted_iota(jnp.int32, x.shape, 0) % CHUNK
    shift = 1
    while shift < CHUNK:
        x = x + jnp.where(row_in_chunk >= shift, pltpu.roll(x, shift, axis=0), 0.0)
        shift *= 2
    return x


N_PROMPT_SCRATCH = 15
N_SAMPLE_SCRATCH = 11


def _prompt_rows(r0, x, nw_ref, win_ref, gb_ref, gnw_ref, rnw_ref, rnb_ref, wo_ref,
                 cos, sin, qdec, kdec, dmat, fnw_ref, y_ref,
                 qd_s, kd_s, ke_s, va_s, ga_s, qb_s, qbd_s, kb_s, kbe_s, vb_s, gbt_s,
                 oa_s, ob_s, s_gla, s_ret, *, final_norm, side_work):
    n_rows = x.shape[0]
    sub = slice(r0, r0 + n_rows)
    gq, gk, gv, gg, la, rq, rk, rv, rg = _project(x, nw_ref, win_ref, gb_ref)
    side_work()

    b = _chunk_cumsum(la)
    chunk_decay_rows = []
    b_last = []
    for c in range(n_rows // CHUNK):
        last = b[(c + 1) * CHUNK - 1:(c + 1) * CHUNK]
        chunk_decay_rows.append(jnp.exp(last))
        b_last.append(jnp.broadcast_to(last, (CHUNK, QK_W)))
    bl = jnp.concatenate(b_last, axis=0)
    qd_s[sub, :] = (gq * jnp.exp(b)).astype(BF16)
    kd_s[sub, :] = (gk * jnp.exp(-b)).astype(BF16)
    ke_s[sub, :] = (gk * jnp.exp(bl - b)).astype(BF16)
    va_s[sub, :] = gv.astype(BF16)
    vb_s[sub, :] = rv.astype(BF16)

    qb = _rope(rq, cos, sin)
    kb = _rope(rk, cos, sin) * (DK ** -0.5)
    qb_s[sub, :] = qb.astype(BF16)
    qbd_s[sub, :] = (qb * qdec).astype(BF16)
    kb_s[sub, :] = kb.astype(BF16)
    kbe_s[sub, :] = (kb * kdec).astype(BF16)
    ga_s[sub, :] = _silu(gg)
    gbt_s[sub, :] = _silu(rg)

    tril = dmat > 0.0
    head_of_lane = lax.broadcasted_iota(jnp.int32, (CHUNK, QK_W), 1) // DK
    n_chunks = n_rows // CHUNK
    chunk_rows = [slice(r0 + c * CHUNK, r0 + (c + 1) * CHUNK) for c in range(n_chunks)]
    att_gla, att_ret = [], []
    for rows in chunk_rows:
        scores = _dot_nt(qd_s[rows, :], _block_diag_keys(kd_s[rows, :], head_of_lane))
        att_gla.append(jnp.where(tril, scores, 0.0).astype(BF16))
        scores = _dot_nt(qb_s[rows, :], _block_diag_keys(kb_s[rows, :], head_of_lane))
        att_ret.append((scores * dmat).astype(BF16))
    kv_gla, kv_ret = [], []
    for rows in chunk_rows:
        ke, v = ke_s[rows, :], _head_blocks(va_s[rows, :])
        kv_gla.append([_dot_tn(ke[:, h * DK:(h + 1) * DK], v[h]) for h in range(HEADS)])
        kbe, vr = kbe_s[rows, :], _head_blocks(vb_s[rows, :])
        kv_ret.append([_dot_tn(kbe[:, h * DK:(h + 1) * DK], vr[h]) for h in range(HEADS)])
    sin_gla, sin_ret = [], []
    for c in range(n_chunks):
        sin_gla.append([s.astype(BF16) for s in s_gla])
        sin_ret.append([s.astype(BF16) for s in s_ret])
        ecol = _col_broadcast(chunk_decay_rows[c])
        s_gla = [ecol[h * DK:(h + 1) * DK] * s_gla[h] + kv_gla[c][h] for h in range(HEADS)]
        s_ret = [_ret_gamma(h) ** CHUNK * s_ret[h] + kv_ret[c][h] for h in range(HEADS)]
    for c, rows in enumerate(chunk_rows):
        oa_s[rows, :] = (_dot(att_gla[c], _block_diag_values(_head_blocks(va_s[rows, :])))
                         + _dot(qd_s[rows, :], _block_diag_values(sin_gla[c])))
        ob_s[rows, :] = (_dot(att_ret[c], _block_diag_values(_head_blocks(vb_s[rows, :])))
                         + _dot(qbd_s[rows, :], _block_diag_values(sin_ret[c])))

    mixed = _head_norm_gate(sub, oa_s, ob_s, ga_s, gbt_s, gnw_ref, rnw_ref, rnb_ref)
    y = x + _dot(mixed, wo_ref[...])
    if final_norm:
        y = _rmsnorm(y, fnw_ref[...])
    y_ref[0, sub, :] = y
    return s_gla, s_ret


def _prompt_tile(x_ref, nw_ref, win_ref, gb_ref, gnw_ref, rnw_ref, rnb_ref, wo_ref,
                 cos_ref, sin_ref, qdec_ref, kdec_ref, dmat_ref, fnw_ref, y_ref,
                 qd_s, kd_s, ke_s, va_s, ga_s, qb_s, qbd_s, kb_s, kbe_s, vb_s, gbt_s,
                 oa_s, ob_s, stg_s, str_s, *, tile, final_norm, side_work):
    dmat = dmat_ref[...]
    s_gla = [stg_s[h] for h in range(HEADS)]
    s_ret = [str_s[h] for h in range(HEADS)]
    block_rows = tile
    for r0 in range(0, tile, block_rows):
        sub = slice(r0, r0 + block_rows)
        s_gla, s_ret = _prompt_rows(
            r0, x_ref[0, sub, :], nw_ref, win_ref, gb_ref, gnw_ref, rnw_ref, rnb_ref, wo_ref,
            cos_ref[sub, :], sin_ref[sub, :], qdec_ref[sub, :], kdec_ref[sub, :], dmat, fnw_ref,
            y_ref, qd_s, kd_s, ke_s, va_s, ga_s, qb_s, qbd_s, kb_s, kbe_s, vb_s, gbt_s,
            oa_s, ob_s, s_gla, s_ret, final_norm=final_norm,
            side_work=side_work if r0 == 0 else (lambda: None))
    for h in range(HEADS):
        stg_s[h] = s_gla[h]
        str_s[h] = s_ret[h]


def _sample_project(x_ref, nw_ref, win_ref, gb_ref, cos_ref, sin_ref,
                    qa_s, ka_s, al_s, va_s, ga_s, qb_s, kb_s, vb_s, gbt_s):
    gq, gk, gv, gg, la, rq, rk, rv, rg = _project(x_ref[...], nw_ref, win_ref, gb_ref)
    qa_s[...] = gq
    ka_s[...] = gk
    al_s[...] = jnp.exp(la)
    va_s[...] = gv
    ga_s[...] = _silu(gg)
    qb_s[...] = _rope(rq, cos_ref[...], sin_ref[...])
    kb_s[...] = _rope(rk, cos_ref[...], sin_ref[...]) * (DK ** -0.5)
    vb_s[...] = rv
    gbt_s[...] = _silu(rg)


def _sample_update(step, block, sg_in, sr_in, sg_out, sr_out,
                   qa_s, ka_s, al_s, va_s, qb_s, kb_s, vb_s, oa_s, ob_s):
    rows = pl.ds(pl.multiple_of(step * block, block), block)
    qa, ka, al, va = qa_s[rows, :], ka_s[rows, :], al_s[rows, :], va_s[rows, :]
    qb, kb, vb = qb_s[rows, :], kb_s[rows, :], vb_s[rows, :]
    oa_rows, ob_rows = [], []
    for j in range(block):
        one = slice(j, j + 1)
        qcol = _col_broadcast(qa[one])
        kcol = _col_broadcast(ka[one])
        acol = _col_broadcast(al[one])
        qrc = _col_broadcast(qb[one])
        krc = _col_broadcast(kb[one])
        oa_h, ob_h = [], []
        for h in range(HEADS):
            ks = slice(h * DK, (h + 1) * DK)
            vs = slice(h * DV, (h + 1) * DV)
            s_new = acol[ks] * sg_in[j, h] + kcol[ks] * va[one, vs]
            sg_out[j, h] = s_new
            oa_h.append(jnp.sum(qcol[ks] * s_new, axis=0, keepdims=True))
            r_new = _ret_gamma(h) * sr_in[j, h] + krc[ks] * vb[one, vs]
            sr_out[j, h] = r_new
            ob_h.append(jnp.sum(qrc[ks] * r_new, axis=0, keepdims=True))
        oa_rows.append(jnp.concatenate(oa_h, axis=1))
        ob_rows.append(jnp.concatenate(ob_h, axis=1))
    oa_s[rows, :] = jnp.concatenate(oa_rows, axis=0)
    ob_s[rows, :] = jnp.concatenate(ob_rows, axis=0)


def _layer_kernel(*refs, tile, block, final_norm, chained):
    (x_ref, nw_ref, win_ref, gb_ref, gnw_ref, rnw_ref, rnb_ref, wo_ref,
     cos_ref, sin_ref, qdec_ref, kdec_ref, dmat_ref, fnw_ref,
     xs_ref, cos_s_ref, sin_s_ref, sg_in, sr_in) = refs[:19]
    refs = refs[19 + (2 if chained else 0):]
    y_ref, sg_ref, sr_ref, ys_ref, sg_out, sr_out = refs[:6]
    prompt_scratch = refs[6:6 + N_PROMPT_SCRATCH]
    (qa_s, ka_s, al_s, vsa_s, gsa_s, qsb_s, ksb_s, vsb_s, gsb_s, osa_s, osb_s) = \
        refs[6 + N_PROMPT_SCRATCH:]
    stg_s, str_s = prompt_scratch[-2:]
    t = pl.program_id(1)
    step = pl.program_id(0) * pl.num_programs(1) + t
    last_step = pl.num_programs(0) * pl.num_programs(1) - 1

    @pl.when(t == 0)
    def _zero_prompt_states():
        stg_s[...] = jnp.zeros_like(stg_s)
        str_s[...] = jnp.zeros_like(str_s)

    @pl.when(step == 0)
    def _project_decode_rows():
        _sample_project(xs_ref, nw_ref, win_ref, gb_ref, cos_s_ref, sin_s_ref,
                        qa_s, ka_s, al_s, vsa_s, gsa_s, qsb_s, ksb_s, vsb_s, gsb_s)

    decode_rows = functools.partial(
        _sample_update, step, block, sg_in, sr_in, sg_out, sr_out,
        qa_s, ka_s, al_s, vsa_s, qsb_s, ksb_s, vsb_s, osa_s, osb_s)
    _prompt_tile(x_ref, nw_ref, win_ref, gb_ref, gnw_ref, rnw_ref, rnb_ref, wo_ref,
                 cos_ref, sin_ref, qdec_ref, kdec_ref, dmat_ref, fnw_ref, y_ref,
                 *prompt_scratch, tile=tile, final_norm=final_norm, side_work=decode_rows)

    @pl.when(t == pl.num_programs(1) - 1)
    def _emit_prompt_states():
        sg_ref[0] = stg_s[...]
        sr_ref[0] = str_s[...]

    @pl.when(step == last_step)
    def _finish_decode_rows():
        mixed = _head_norm_gate(slice(None), osa_s, osb_s, gsa_s, gsb_s, gnw_ref, rnw_ref,
                                rnb_ref)
        ys = xs_ref[...] + _dot(mixed, wo_ref[...])
        if final_norm:
            ys = _rmsnorm(ys, fnw_ref[...])
        ys_ref[...] = ys


def _const_spec(shape):
    nd = len(shape)
    return pl.BlockSpec(shape, lambda *_: (0,) * nd)


def _layer_spec(shape, layer):
    nd = len(shape)
    return pl.BlockSpec((None,) + tuple(shape[1:]), lambda *_: (layer,) + (0,) * (nd - 1))


def _layer(x, xs, wts, tables, sample_tables, fnw, sg, sr, prev_out, *, layer, final_norm):
    bsz, seq, d = x.shape
    n = xs.shape[0]
    tile = min(PROMPT_TILE, seq)
    assert seq % tile == 0
    nt = seq // tile
    steps = bsz * nt
    block = n // steps
    assert block * steps == n and block % SUBLANES == 0, "decode rows must split evenly over steps"
    cos, sin, qdec, kdec, dmat = tables
    cos_s, sin_s = sample_tables
    tile_spec = pl.BlockSpec((1, tile, d), lambda b, t: (b, t, 0))
    in_specs = [tile_spec] + [_layer_spec(a.shape, layer) for a in wts]
    in_specs += [pl.BlockSpec((tile, LANES), lambda b, t: (t, 0)),
                 pl.BlockSpec((tile, LANES), lambda b, t: (t, 0)),
                 _const_spec(qdec.shape), _const_spec(kdec.shape), _const_spec(dmat.shape),
                 _const_spec(fnw.shape), _const_spec(xs.shape), _const_spec(cos_s.shape),
                 _const_spec(sin_s.shape)]
    st_spec = pl.BlockSpec((None, block, HEADS, DK, DV), lambda b, t: (layer, b * nt + t, 0, 0, 0))
    in_specs += [st_spec, st_spec]
    operands = [x, *wts, cos, sin, qdec, kdec, dmat, fnw, xs, cos_s, sin_s, sg, sr]
    aliases = {}
    if prev_out is not None:
        in_specs += [pl.BlockSpec(memory_space=pl.ANY)] * 2
        aliases = {len(operands): 4, len(operands) + 1: 5}
        operands += list(prev_out)
    pstate_shape = jax.ShapeDtypeStruct((bsz, HEADS, DK, DV), F32)
    pstate_spec = pl.BlockSpec((1, HEADS, DK, DV), lambda b, t: (b, 0, 0, 0))
    prompt_scratch = (
        [pltpu.VMEM((tile, QK_W), BF16)] * 3
        + [pltpu.VMEM((tile, V_W), BF16)]
        + [pltpu.VMEM((tile, V_W), F32)]
        + [pltpu.VMEM((tile, QK_W), BF16)] * 4
        + [pltpu.VMEM((tile, V_W), BF16)]
        + [pltpu.VMEM((tile, V_W), F32)] * 3
        + [pltpu.VMEM((HEADS, DK, DV), F32)] * 2
    )
    sample_scratch = (
        [pltpu.VMEM((n, QK_W), F32)] * 3
        + [pltpu.VMEM((n, V_W), F32)] * 2
        + [pltpu.VMEM((n, QK_W), F32)] * 2
        + [pltpu.VMEM((n, V_W), F32)] * 4
    )
    assert len(prompt_scratch) == N_PROMPT_SCRATCH and len(sample_scratch) == N_SAMPLE_SCRATCH
    y, sgp, srp, ys, sg_new, sr_new = pl.pallas_call(
        functools.partial(_layer_kernel, tile=tile, block=block, final_norm=final_norm,
                          chained=prev_out is not None),
        grid=(bsz, nt),
        in_specs=in_specs,
        out_specs=[tile_spec, pstate_spec, pstate_spec, _const_spec(xs.shape), st_spec, st_spec],
        out_shape=[jax.ShapeDtypeStruct(x.shape, F32), pstate_shape, pstate_shape,
                   jax.ShapeDtypeStruct(xs.shape, F32),
                   jax.ShapeDtypeStruct(sg.shape, F32), jax.ShapeDtypeStruct(sr.shape, F32)],
        input_output_aliases=aliases,
        scratch_shapes=prompt_scratch + sample_scratch,
        compiler_params=pltpu.CompilerParams(
            dimension_semantics=("arbitrary", "arbitrary"),
            vmem_limit_bytes=VMEM_LIMIT_BYTES),
        name="hybrid_layer",
    )(*operands)
    return y, sgp, srp, ys, (sg_new, sr_new)


def _rope_tables(pos):
    half = DK // 2
    inv = ROPE_BASE ** (-np.arange(half, dtype=np.float64) / half)
    ang = np.asarray(pos, dtype=np.float64)[:, None] * inv[None, :]
    cos = np.tile(np.cos(ang), (1, LANES // half))
    sin = np.tile(np.concatenate([-np.sin(ang), np.sin(ang)], axis=1), (1, LANES // DK))
    return jnp.asarray(cos, F32), jnp.asarray(sin, F32)


def _retention_tables(tile):
    idx = np.arange(CHUNK, dtype=np.float64)
    gam = np.array([_ret_gamma(h) for h in range(HEADS)], dtype=np.float64)
    qdec = np.repeat(gam[None, :] ** (idx[:, None] + 1.0), DK, axis=1)
    kdec = np.repeat(gam[None, :] ** (CHUNK - 1.0 - idx[:, None]), DK, axis=1)
    rel = idx[:, None] - idx[None, :]
    dm = [np.where(rel >= 0, gam[h] ** np.maximum(rel, 0.0), 0.0) for h in range(HEADS)]
    dmat = np.concatenate(dm, axis=1)
    reps = tile // CHUNK
    return (jnp.asarray(np.tile(qdec, (reps, 1)), F32), jnp.asarray(np.tile(kdec, (reps, 1)), F32),
            jnp.asarray(dmat, F32))


def _fold_weights_kernel(wt_ref, w2_ref, wout_ref, fused_ref, wout_bf_ref):
    a_hi, a_lo = _split_hi_lo(wt_ref[A_COLS:A_COLS + RANK, :])
    b_hi, b_lo = _split_hi_lo(w2_ref[...])
    folded = _dot_tn(a_hi, b_hi) + _dot_tn(a_hi, b_lo) + _dot_tn(a_lo, b_hi)
    fused_ref[:, 0:QK_W] = folded.astype(BF16)
    for j in range(1, WIN_COLS // FOLD_COLS):
        start = (j - 1) * FOLD_COLS + (RANK if j > A_COLS // FOLD_COLS else 0)
        fused_ref[:, j * FOLD_COLS:(j + 1) * FOLD_COLS] = \
            wt_ref[start:start + FOLD_COLS, :].astype(BF16).T
    wout_bf_ref[...] = wout_ref[...].astype(BF16)


def _fold_weights(w_in, gla_w2, w_out):
    depth, d, in_cols = w_in.shape
    assert in_cols == 2 * A_COLS + RANK and A_COLS % FOLD_COLS == 0 and QK_W == FOLD_COLS
    assert d % FOLD_ROWS == 0
    w_in_t = jnp.swapaxes(w_in, 1, 2)
    d_out = w_out.shape[2]
    return pl.pallas_call(
        _fold_weights_kernel,
        grid=(depth, d // FOLD_ROWS),
        in_specs=[pl.BlockSpec((None, in_cols, FOLD_ROWS), lambda l, i: (l, 0, i)),
                  pl.BlockSpec((None, RANK, QK_W), lambda l, i: (l, 0, 0)),
                  pl.BlockSpec((None, FOLD_ROWS, d_out), lambda l, i: (l, i, 0))],
        out_specs=[pl.BlockSpec((None, FOLD_ROWS, WIN_COLS), lambda l, i: (l, i, 0)),
                   pl.BlockSpec((None, FOLD_ROWS, d_out), lambda l, i: (l, i, 0))],
        out_shape=[jax.ShapeDtypeStruct((depth, d, WIN_COLS), BF16),
                   jax.ShapeDtypeStruct(w_out.shape, BF16)],
        compiler_params=pltpu.CompilerParams(dimension_semantics=("arbitrary", "arbitrary")),
        name="fold_weights",
    )(w_in_t, gla_w2, w_out)


def _stacked_params(fused_in, wout_bf, norm_w, gla_b, gla_norm_w, ret_norm_w, ret_norm_b):
    row = lambda p: p[:, None, :]
    return (row(norm_w), fused_in, row(gla_b), row(gla_norm_w), row(ret_norm_w), row(ret_norm_b),
            wout_bf)


def kernel(x_prompt, x_sample, state_gla, state_ret, norm_w, w_in, gla_w2, gla_b, gla_norm_w,
           ret_norm_w, ret_norm_b, w_out, final_norm_w):
    bp, tp, d = x_prompt.shape
    bs, ts, _ = x_sample.shape
    assert ts == 1, "the decode path handles one new token per sequence"
    depth = w_in.shape[0]
    tile = min(PROMPT_TILE, tp)
    prompt_tables = _rope_tables(np.arange(tp)) + _retention_tables(tile)
    sample_tables = _rope_tables(PAST_LEN + np.arange(ts))
    fnw = final_norm_w[None, :]

    hp = x_prompt
    hs = x_sample.reshape(bs, d)
    gla_p, ret_p = [], []
    sample_states = None
    fused_in, wout_bf = _fold_weights(w_in, gla_w2, w_out)
    wts = _stacked_params(fused_in, wout_bf, norm_w, gla_b, gla_norm_w, ret_norm_w, ret_norm_b)
    for l in range(depth):
        hp, sg, sr, hs, sample_states = _layer(
            hp, hs, wts, prompt_tables, sample_tables, fnw, state_gla, state_ret, sample_states,
            layer=l, final_norm=l == depth - 1)
        gla_p.append(sg)
        ret_p.append(sr)
    return (hp, hs.reshape(bs, ts, d), jnp.stack(gla_p), jnp.stack(ret_p),
            sample_states[0], sample_states[1])
```

```python
import functools

import numpy as np
import jax
import jax.numpy as jnp
from jax import lax
from jax.experimental import pallas as pl
from jax.experimental.pallas import tpu as pltpu

F32 = jnp.float32
BF16 = jnp.bfloat16

HEADS = 4
DK = 64
DV = 128
QK_W = HEADS * DK
V_W = HEADS * DV
RANK = 16
GATE_TAU = 16.0
CHUNK = 64
ROPE_BASE = 10000.0
PAST_LEN = 16384
EPS = 1e-6
GN_EPS = 1e-5
A_COLS = 2 * QK_W + 2 * V_W
LANES = 128
WIN_COLS = 2 * A_COLS + QK_W
FOLD_COLS = 256
FOLD_ROWS = 256
SUBLANES = 8
VMEM_LIMIT_BYTES = 58 * 1024 * 1024

PROMPT_TILE = 1024
PROJ_ROWS = 256


def _dot(a, b):
    return jnp.dot(a, b, preferred_element_type=F32)


def _dot_nt(a, b):
    return lax.dot_general(a, b, (((1,), (1,)), ((), ())), preferred_element_type=F32)


def _dot_tn(a, b):
    return lax.dot_general(a, b, (((0,), (0,)), ((), ())), preferred_element_type=F32)


def _rmsnorm(x, w):
    return x * lax.rsqrt(jnp.mean(x * x, axis=-1, keepdims=True) + EPS) * w


def _silu(x):
    return x * (1.0 / (1.0 + jnp.exp(-x)))


def _log_sigmoid(x):
    return jnp.minimum(x, 0.0) - jnp.log1p(jnp.exp(-jnp.abs(x)))


def _ret_gamma(h):
    return 1.0 - 2.0 ** (-5.0 - h)


def _rope(x, cos, sin_signed):
    lane = lax.broadcasted_iota(jnp.int32, (x.shape[0], LANES), 1)
    first_half = (lane % DK) < (DK // 2)
    outs = []
    for j in range(QK_W // LANES):
        xs = x[:, j * LANES:(j + 1) * LANES]
        swapped = jnp.where(first_half,
                            pltpu.roll(xs, LANES - DK // 2, axis=1),
                            pltpu.roll(xs, DK // 2, axis=1))
        outs.append(xs * cos + swapped * sin_signed)
    return jnp.concatenate(outs, axis=1)


def _split_hi_lo(x):
    hi = x.astype(BF16)
    lo = (x - hi.astype(F32)).astype(BF16)
    return hi, lo


def _col_broadcast(row):
    return jnp.broadcast_to(row, (LANES, row.shape[1])).T


def _dup_transpose(k):
    parts = []
    for c in range(k.shape[0] // CHUNK):
        blk = k[c * CHUNK:(c + 1) * CHUNK]
        parts += [blk, blk]
    return jnp.concatenate(parts, axis=0).T.astype(BF16)


def _block_diag_keys_t(kt2, same_head):
    kt4 = jnp.concatenate([kt2, kt2], axis=1)
    return jnp.where(same_head, kt4, jnp.zeros_like(kt4))


def _pair_dot(lhs, blocks):
    z = jnp.zeros_like(blocks[0])
    outs = []
    for p in range(0, HEADS, 2):
        rhs = jnp.concatenate([jnp.concatenate([blocks[p], z], axis=1),
                               jnp.concatenate([z, blocks[p + 1]], axis=1)], axis=0)
        outs.append(_dot(lhs[:, p * DK:(p + 2) * DK], rhs))
    return jnp.concatenate(outs, axis=1)


def _head_blocks(v):
    return [v[:, h * DV:(h + 1) * DV] for h in range(HEADS)]


def _head_norm_gate(rows, oa_ref, ob_ref, ga_ref, gb_ref, gnw_ref, rnw_ref, rnb_ref):
    parts = []
    for j in range(HEADS):
        sl = slice(j * DV, (j + 1) * DV)
        o = oa_ref[rows, sl]
        y = o * lax.rsqrt(jnp.mean(o * o, axis=-1, keepdims=True) + EPS) * gnw_ref[...]
        parts.append((y * ga_ref[rows, sl]).astype(BF16))
    for j in range(HEADS):
        sl = slice(j * DV, (j + 1) * DV)
        o = ob_ref[rows, sl]
        d = o - jnp.mean(o, axis=-1, keepdims=True)
        var = jnp.mean(d * d, axis=-1, keepdims=True)
        y = (d * lax.rsqrt(var + GN_EPS)) * rnw_ref[:, sl] + rnb_ref[:, sl]
        parts.append((y * gb_ref[rows, sl]).astype(BF16))
    return jnp.concatenate(parts, axis=1)


def _project(x, nw_ref, win_ref, gb_ref):
    pres, zs = [], []
    for r in range(0, x.shape[0], PROJ_ROWS):
        hb = _rmsnorm(x[r:r + PROJ_ROWS], nw_ref[...]).astype(BF16)
        pres.append(_dot(hb, win_ref[:, 0:QK_W]))
        zs.append(_dot(hb, win_ref[:, QK_W:]))
    pre = jnp.concatenate(pres, axis=0)
    z = jnp.concatenate(zs, axis=0)
    la = _log_sigmoid(pre + gb_ref[...]) * (1.0 / GATE_TAU)
    g0, r0 = 0, A_COLS
    gq = z[:, g0:g0 + QK_W] * (DK ** -0.5)
    gk = z[:, g0 + QK_W:g0 + 2 * QK_W]
    gv = z[:, g0 + 2 * QK_W:g0 + 2 * QK_W + V_W]
    gg = z[:, g0 + 2 * QK_W + V_W:r0]
    rq = z[:, r0:r0 + QK_W]
    rk = z[:, r0 + QK_W:r0 + 2 * QK_W]
    rv = z[:, r0 + 2 * QK_W:r0 + 2 * QK_W + V_W]
    rg = z[:, r0 + 2 * QK_W + V_W:2 * A_COLS]
    return gq, gk, gv, gg, la, rq, rk, rv, rg


def _chunk_cumsum(x):
    row_in_chunk = lax.broadcasted_iota(jnp.int32, x.shape, 0) % CHUNK
    shift = 1
    while shift < CHUNK:
        x = x + jnp.where(row_in_chunk >= shift, pltpu.roll(x, shift, axis=0), 0.0)
        shift *= 2
    return x


N_PROMPT_SCRATCH = 15
N_SAMPLE_SCRATCH = 11


def _prompt_rows(r0, x, nw_ref, win_ref, gb_ref, gnw_ref, rnw_ref, rnb_ref, wo_ref,
                 cos, sin, qdec, kdec, dmat, fnw_ref, y_ref,
                 qd_s, kd_s, ke_s, va_s, ga_s, qb_s, qbd_s, kb_s, kbe_s, vb_s, gbt_s,
                 oa_s, ob_s, s_gla, s_ret, *, final_norm, side_work):
    n_rows = x.shape[0]
    sub = slice(r0, r0 + n_rows)
    dup = slice(2 * r0, 2 * (r0 + n_rows))
    gq, gk, gv, gg, la, rq, rk, rv, rg = _project(x, nw_ref, win_ref, gb_ref)
    side_work()

    b = _chunk_cumsum(la)
    chunk_decay_rows = []
    b_last = []
    for c in range(n_rows // CHUNK):
        last = b[(c + 1) * CHUNK - 1:(c + 1) * CHUNK]
        chunk_decay_rows.append(jnp.exp(last))
        b_last.append(jnp.broadcast_to(last, (CHUNK, QK_W)))
    bl = jnp.concatenate(b_last, axis=0)
    qd_s[sub, :] = (gq * jnp.exp(b)).astype(BF16)
    kd_s[:, dup] = _dup_transpose(gk * jnp.exp(-b))
    ke_s[sub, :] = (gk * jnp.exp(bl - b)).astype(BF16)
    va_s[sub, :] = gv.astype(BF16)
    vb_s[sub, :] = rv.astype(BF16)

    qb = _rope(rq, cos, sin)
    kb = _rope(rk, cos, sin) * (DK ** -0.5)
    qb_s[sub, :] = qb.astype(BF16)
    qbd_s[sub, :] = (qb * qdec).astype(BF16)
    kb_s[:, dup] = _dup_transpose(kb)
    kbe_s[sub, :] = (kb * kdec).astype(BF16)
    ga_s[sub, :] = _silu(gg)
    gbt_s[sub, :] = _silu(rg)

    tril = dmat > 0.0
    same_head = (lax.broadcasted_iota(jnp.int32, (QK_W, HEADS * CHUNK), 0) // DK
                 == lax.broadcasted_iota(jnp.int32, (QK_W, HEADS * CHUNK), 1) // CHUNK)
    n_chunks = n_rows // CHUNK
    chunk_rows = [slice(r0 + c * CHUNK, r0 + (c + 1) * CHUNK) for c in range(n_chunks)]
    att_gla, att_ret = [], []
    for rows in chunk_rows:
        cols = slice(2 * rows.start, 2 * rows.stop)
        scores = _dot(qd_s[rows, :], _block_diag_keys_t(kd_s[:, cols], same_head))
        att_gla.append(jnp.where(tril, scores, 0.0).astype(BF16))
        scores = _dot(qb_s[rows, :], _block_diag_keys_t(kb_s[:, cols], same_head))
        att_ret.append((scores * dmat).astype(BF16))
    kv_gla, kv_ret = [], []
    for rows in chunk_rows:
        ke, v = ke_s[rows, :], _head_blocks(va_s[rows, :])
        kv_gla.append([_dot_tn(ke[:, h * DK:(h + 1) * DK], v[h]) for h in range(HEADS)])
        kbe, vr = kbe_s[rows, :], _head_blocks(vb_s[rows, :])
        kv_ret.append([_dot_tn(kbe[:, h * DK:(h + 1) * DK], vr[h]) for h in range(HEADS)])
    sin_gla, sin_ret = [], []
    for c in range(n_chunks):
        sin_gla.append([s.astype(BF16) for s in s_gla])
        sin_ret.append([s.astype(BF16) for s in s_ret])
        ecol = _col_broadcast(chunk_decay_rows[c])
        s_gla = [ecol[h * DK:(h + 1) * DK] * s_gla[h] + kv_gla[c][h] for h in range(HEADS)]
        s_ret = [_ret_gamma(h) ** CHUNK * s_ret[h] + kv_ret[c][h] for h in range(HEADS)]
    for c, rows in enumerate(chunk_rows):
        oa_s[rows, :] = (_pair_dot(att_gla[c], _head_blocks(va_s[rows, :]))
                         + _pair_dot(qd_s[rows, :], sin_gla[c]))
        ob_s[rows, :] = (_pair_dot(att_ret[c], _head_blocks(vb_s[rows, :]))
                         + _pair_dot(qbd_s[rows, :], sin_ret[c]))

    mixed = _head_norm_gate(sub, oa_s, ob_s, ga_s, gbt_s, gnw_ref, rnw_ref, rnb_ref)
    y = x + _dot(mixed, wo_ref[...])
    if final_norm:
        y = _rmsnorm(y, fnw_ref[...])
    y_ref[0, sub, :] = y
    return s_gla, s_ret


def _prompt_tile(x_ref, nw_ref, win_ref, gb_ref, gnw_ref, rnw_ref, rnb_ref, wo_ref,
                 cos_ref, sin_ref, qdec_ref, kdec_ref, dmat_ref, fnw_ref, y_ref,
                 qd_s, kd_s, ke_s, va_s, ga_s, qb_s, qbd_s, kb_s, kbe_s, vb_s, gbt_s,
                 oa_s, ob_s, stg_s, str_s, *, tile, final_norm, side_work):
    dmat = dmat_ref[...]
    s_gla = [stg_s[h] for h in range(HEADS)]
    s_ret = [str_s[h] for h in range(HEADS)]
    block_rows = tile
    for r0 in range(0, tile, block_rows):
        sub = slice(r0, r0 + block_rows)
        s_gla, s_ret = _prompt_rows(
            r0, x_ref[0, sub, :], nw_ref, win_ref, gb_ref, gnw_ref, rnw_ref, rnb_ref, wo_ref,
            cos_ref[sub, :], sin_ref[sub, :], qdec_ref[sub, :], kdec_ref[sub, :], dmat, fnw_ref,
            y_ref, qd_s, kd_s, ke_s, va_s, ga_s, qb_s, qbd_s, kb_s, kbe_s, vb_s, gbt_s,
            oa_s, ob_s, s_gla, s_ret, final_norm=final_norm,
            side_work=side_work if r0 == 0 else (lambda: None))
    for h in range(HEADS):
        stg_s[h] = s_gla[h]
        str_s[h] = s_ret[h]


def _sample_project(x_ref, nw_ref, win_ref, gb_ref, cos_ref, sin_ref,
                    qa_s, ka_s, al_s, va_s, ga_s, qb_s, kb_s, vb_s, gbt_s):
    gq, gk, gv, gg, la, rq, rk, rv, rg = _project(x_ref[...], nw_ref, win_ref, gb_ref)
    qa_s[...] = gq
    ka_s[...] = gk
    al_s[...] = jnp.exp(la)
    va_s[...] = gv
    ga_s[...] = _silu(gg)
    qb_s[...] = _rope(rq, cos_ref[...], sin_ref[...])
    kb_s[...] = _rope(rk, cos_ref[...], sin_ref[...]) * (DK ** -0.5)
    vb_s[...] = rv
    gbt_s[...] = _silu(rg)


def _sample_update(step, block, sg_in, sr_in, sg_out, sr_out,
                   qa_s, ka_s, al_s, va_s, qb_s, kb_s, vb_s, oa_s, ob_s):
    rows = pl.ds(pl.multiple_of(step * block, block), block)
    qa, ka, al, va = qa_s[rows, :], ka_s[rows, :], al_s[rows, :], va_s[rows, :]
    qb, kb, vb = qb_s[rows, :], kb_s[rows, :], vb_s[rows, :]
    oa_rows, ob_rows = [], []
    for j in range(block):
        one = slice(j, j + 1)
        qcol = _col_broadcast(qa[one])
        kcol = _col_broadcast(ka[one])
        acol = _col_broadcast(al[one])
        qrc = _col_broadcast(qb[one])
        krc = _col_broadcast(kb[one])
        oa_h, ob_h = [], []
        for h in range(HEADS):
            ks = slice(h * DK, (h + 1) * DK)
            vs = slice(h * DV, (h + 1) * DV)
            s_new = acol[ks] * sg_in[j, h] + kcol[ks] * va[one, vs]
            sg_out[j, h] = s_new
            oa_h.append(jnp.sum(qcol[ks] * s_new, axis=0, keepdims=True))
            r_new = _ret_gamma(h) * sr_in[j, h] + krc[ks] * vb[one, vs]
            sr_out[j, h] = r_new
            ob_h.append(jnp.sum(qrc[ks] * r_new, axis=0, keepdims=True))
        oa_rows.append(jnp.concatenate(oa_h, axis=1))
        ob_rows.append(jnp.concatenate(ob_h, axis=1))
    oa_s[rows, :] = jnp.concatenate(oa_rows, axis=0)
    ob_s[rows, :] = jnp.concatenate(ob_rows, axis=0)


def _layer_kernel(*refs, tile, block, final_norm, chained):
    (x_ref, nw_ref, win_ref, gb_ref, gnw_ref, rnw_ref, rnb_ref, wo_ref,
     cos_ref, sin_ref, qdec_ref, kdec_ref, dmat_ref, fnw_ref,
     xs_ref, cos_s_ref, sin_s_ref, sg_in, sr_in) = refs[:19]
    refs = refs[19 + (2 if chained else 0):]
    y_ref, sg_ref, sr_ref, ys_ref, sg_out, sr_out = refs[:6]
    prompt_scratch = refs[6:6 + N_PROMPT_SCRATCH]
    (qa_s, ka_s, al_s, vsa_s, gsa_s, qsb_s, ksb_s, vsb_s, gsb_s, osa_s, osb_s) = \
        refs[6 + N_PROMPT_SCRATCH:]
    stg_s, str_s = prompt_scratch[-2:]
    t = pl.program_id(1)
    step = pl.program_id(0) * pl.num_programs(1) + t
    last_step = pl.num_programs(0) * pl.num_programs(1) - 1

    @pl.when(t == 0)
    def _zero_prompt_states():
        stg_s[...] = jnp.zeros_like(stg_s)
        str_s[...] = jnp.zeros_like(str_s)

    @pl.when(step == 0)
    def _project_decode_rows():
        _sample_project(xs_ref, nw_ref, win_ref, gb_ref, cos_s_ref, sin_s_ref,
                        qa_s, ka_s, al_s, vsa_s, gsa_s, qsb_s, ksb_s, vsb_s, gsb_s)

    decode_rows = functools.partial(
        _sample_update, step, block, sg_in, sr_in, sg_out, sr_out,
        qa_s, ka_s, al_s, vsa_s, qsb_s, ksb_s, vsb_s, osa_s, osb_s)
    _prompt_tile(x_ref, nw_ref, win_ref, gb_ref, gnw_ref, rnw_ref, rnb_ref, wo_ref,
                 cos_ref, sin_ref, qdec_ref, kdec_ref, dmat_ref, fnw_ref, y_ref,
                 *prompt_scratch, tile=tile, final_norm=final_norm, side_work=decode_rows)

    @pl.when(t == pl.num_programs(1) - 1)
    def _emit_prompt_states():
        sg_ref[0] = stg_s[...]
        sr_ref[0] = str_s[...]

    @pl.when(step == last_step)
    def _finish_decode_rows():
        mixed = _head_norm_gate(slice(None), osa_s, osb_s, gsa_s, gsb_s, gnw_ref, rnw_ref,
                                rnb_ref)
        ys = xs_ref[...] + _dot(mixed, wo_ref[...])
        if final_norm:
            ys = _rmsnorm(ys, fnw_ref[...])
        ys_ref[...] = ys


def _const_spec(shape):
    nd = len(shape)
    return pl.BlockSpec(shape, lambda *_: (0,) * nd)


def _layer_spec(shape, layer):
    nd = len(shape)
    return pl.BlockSpec((None,) + tuple(shape[1:]), lambda *_: (layer,) + (0,) * (nd - 1))


def _layer(x, xs, wts, tables, sample_tables, fnw, sg, sr, prev_out, *, layer, final_norm):
    bsz, seq, d = x.shape
    n = xs.shape[0]
    tile = min(PROMPT_TILE, seq)
    assert seq % tile == 0
    nt = seq // tile
    steps = bsz * nt
    block = n // steps
    assert block * steps == n and block % SUBLANES == 0, "decode rows must split evenly over steps"
    cos, sin, qdec, kdec, dmat = tables
    cos_s, sin_s = sample_tables
    tile_spec = pl.BlockSpec((1, tile, d), lambda b, t: (b, t, 0))
    in_specs = [tile_spec] + [_layer_spec(a.shape, layer) for a in wts]
    in_specs += [pl.BlockSpec((tile, LANES), lambda b, t: (t, 0)),
                 pl.BlockSpec((tile, LANES), lambda b, t: (t, 0)),
                 _const_spec(qdec.shape), _const_spec(kdec.shape), _const_spec(dmat.shape),
                 _const_spec(fnw.shape), _const_spec(xs.shape), _const_spec(cos_s.shape),
                 _const_spec(sin_s.shape)]
    st_spec = pl.BlockSpec((None, block, HEADS, DK, DV), lambda b, t: (layer, b * nt + t, 0, 0, 0))
    in_specs += [st_spec, st_spec]
    operands = [x, *wts, cos, sin, qdec, kdec, dmat, fnw, xs, cos_s, sin_s, sg, sr]
    aliases = {}
    if prev_out is not None:
        in_specs += [pl.BlockSpec(memory_space=pl.ANY)] * 2
        aliases = {len(operands): 4, len(operands) + 1: 5}
        operands += list(prev_out)
    pstate_shape = jax.ShapeDtypeStruct((bsz, HEADS, DK, DV), F32)
    pstate_spec = pl.BlockSpec((1, HEADS, DK, DV), lambda b, t: (b, 0, 0, 0))
    prompt_scratch = (
        [pltpu.VMEM((tile, QK_W), BF16)]
        + [pltpu.VMEM((QK_W, 2 * tile), BF16)]
        + [pltpu.VMEM((tile, QK_W), BF16)]
        + [pltpu.VMEM((tile, V_W), BF16)]
        + [pltpu.VMEM((tile, V_W), F32)]
        + [pltpu.VMEM((tile, QK_W), BF16)] * 2
        + [pltpu.VMEM((QK_W, 2 * tile), BF16)]
        + [pltpu.VMEM((tile, QK_W), BF16)]
        + [pltpu.VMEM((tile, V_W), BF16)]
        + [pltpu.VMEM((tile, V_W), F32)] * 3
        + [pltpu.VMEM((HEADS, DK, DV), F32)] * 2
    )
    sample_scratch = (
        [pltpu.VMEM((n, QK_W), F32)] * 3
        + [pltpu.VMEM((n, V_W), F32)] * 2
        + [pltpu.VMEM((n, QK_W), F32)] * 2
        + [pltpu.VMEM((n, V_W), F32)] * 4
    )
    assert len(prompt_scratch) == N_PROMPT_SCRATCH and len(sample_scratch) == N_SAMPLE_SCRATCH
    y, sgp, srp, ys, sg_new, sr_new = pl.pallas_call(
        functools.partial(_layer_kernel, tile=tile, block=block, final_norm=final_norm,
                          chained=prev_out is not None),
        grid=(bsz, nt),
        in_specs=in_specs,
        out_specs=[tile_spec, pstate_spec, pstate_spec, _const_spec(xs.shape), st_spec, st_spec],
        out_shape=[jax.ShapeDtypeStruct(x.shape, F32), pstate_shape, pstate_shape,
                   jax.ShapeDtypeStruct(xs.shape, F32),
                   jax.ShapeDtypeStruct(sg.shape, F32), jax.ShapeDtypeStruct(sr.shape, F32)],
        input_output_aliases=aliases,
        scratch_shapes=prompt_scratch + sample_scratch,
        compiler_params=pltpu.CompilerParams(
            dimension_semantics=("arbitrary", "arbitrary"),
            vmem_limit_bytes=VMEM_LIMIT_BYTES),
        name="hybrid_layer",
    )(*operands)
    return y, sgp, srp, ys, (sg_new, sr_new)


def _rope_tables(pos):
    half = DK // 2
    inv = ROPE_BASE ** (-np.arange(half, dtype=np.float64) / half)
    ang = np.asarray(pos, dtype=np.float64)[:, None] * inv[None, :]
    cos = np.tile(np.cos(ang), (1, LANES // half))
    sin = np.tile(np.concatenate([-np.sin(ang), np.sin(ang)], axis=1), (1, LANES // DK))
    return jnp.asarray(cos, F32), jnp.asarray(sin, F32)


def _retention_tables(tile):
    idx = np.arange(CHUNK, dtype=np.float64)
    gam = np.array([_ret_gamma(h) for h in range(HEADS)], dtype=np.float64)
    qdec = np.repeat(gam[None, :] ** (idx[:, None] + 1.0), DK, axis=1)
    kdec = np.repeat(gam[None, :] ** (CHUNK - 1.0 - idx[:, None]), DK, axis=1)
    rel = idx[:, None] - idx[None, :]
    dm = [np.where(rel >= 0, gam[h] ** np.maximum(rel, 0.0), 0.0) for h in range(HEADS)]
    dmat = np.concatenate(dm, axis=1)
    reps = tile // CHUNK
    return (jnp.asarray(np.tile(qdec, (reps, 1)), F32), jnp.asarray(np.tile(kdec, (reps, 1)), F32),
            jnp.asarray(dmat, F32))


def _fold_weights_kernel(wt_ref, w2_ref, wout_ref, fused_ref, wout_bf_ref):
    a_hi, a_lo = _split_hi_lo(wt_ref[A_COLS:A_COLS + RANK, :])
    b_hi, b_lo = _split_hi_lo(w2_ref[...])
    folded = _dot_tn(a_hi, b_hi) + _dot_tn(a_hi, b_lo) + _dot_tn(a_lo, b_hi)
    fused_ref[:, 0:QK_W] = folded.astype(BF16)
    for j in range(1, WIN_COLS // FOLD_COLS):
        start = (j - 1) * FOLD_COLS + (RANK if j > A_COLS // FOLD_COLS else 0)
        fused_ref[:, j * FOLD_COLS:(j + 1) * FOLD_COLS] = \
            wt_ref[start:start + FOLD_COLS, :].astype(BF16).T
    wout_bf_ref[...] = wout_ref[...].astype(BF16)


def _fold_weights(w_in, gla_w2, w_out):
    depth, d, in_cols = w_in.shape
    assert in_cols == 2 * A_COLS + RANK and A_COLS % FOLD_COLS == 0 and QK_W == FOLD_COLS
    assert d % FOLD_ROWS == 0
    w_in_t = jnp.swapaxes(w_in, 1, 2)
    d_out = w_out.shape[2]
    return pl.pallas_call(
        _fold_weights_kernel,
        grid=(depth, d // FOLD_ROWS),
        in_specs=[pl.BlockSpec((None, in_cols, FOLD_ROWS), lambda l, i: (l, 0, i)),
                  pl.BlockSpec((None, RANK, QK_W), lambda l, i: (l, 0, 0)),
                  pl.BlockSpec((None, FOLD_ROWS, d_out), lambda l, i: (l, i, 0))],
        out_specs=[pl.BlockSpec((None, FOLD_ROWS, WIN_COLS), lambda l, i: (l, i, 0)),
                   pl.BlockSpec((None, FOLD_ROWS, d_out), lambda l, i: (l, i, 0))],
        out_shape=[jax.ShapeDtypeStruct((depth, d, WIN_COLS), BF16),
                   jax.ShapeDtypeStruct(w_out.shape, BF16)],
        compiler_params=pltpu.CompilerParams(dimension_semantics=("arbitrary", "arbitrary")),
        name="fold_weights",
    )(w_in_t, gla_w2, w_out)


def _stacked_params(fused_in, wout_bf, norm_w, gla_b, gla_norm_w, ret_norm_w, ret_norm_b):
    row = lambda p: p[:, None, :]
    return (row(norm_w), fused_in, row(gla_b), row(gla_norm_w), row(ret_norm_w), row(ret_norm_b),
            wout_bf)


def kernel(x_prompt, x_sample, state_gla, state_ret, norm_w, w_in, gla_w2, gla_b, gla_norm_w,
           ret_norm_w, ret_norm_b, w_out, final_norm_w):
    bp, tp, d = x_prompt.shape
    bs, ts, _ = x_sample.shape
    assert ts == 1, "the decode path handles one new token per sequence"
    depth = w_in.shape[0]
    tile = min(PROMPT_TILE, tp)
    prompt_tables = _rope_tables(np.arange(tp)) + _retention_tables(tile)
    sample_tables = _rope_tables(PAST_LEN + np.arange(ts))
    fnw = final_norm_w[None, :]

    hp = x_prompt
    hs = x_sample.reshape(bs, d)
    gla_p, ret_p = [], []
    sample_states = None
    fused_in, wout_bf = _fold_weights(w_in, gla_w2, w_out)
    wts = _stacked_params(fused_in, wout_bf, norm_w, gla_b, gla_norm_w, ret_norm_w, ret_norm_b)
    for l in range(depth):
        hp, sg, sr, hs, sample_states = _layer(
            hp, hs, wts, prompt_tables, sample_tables, fnw, state_gla, state_ret, sample_states,
            layer=l, final_norm=l == depth - 1)
        gla_p.append(sg)
        ret_p.append(sr)
    return (hp, hs.reshape(bs, ts, d), jnp.stack(gla_p), jnp.stack(ret_p),
            sample_states[0], sample_states[1])
```

```python
import functools

import numpy as np
import jax
import jax.numpy as jnp
from jax import lax
from jax.experimental import pallas as pl
from jax.experimental.pallas import tpu as pltpu

F32 = jnp.float32
BF16 = jnp.bfloat16

HEADS = 4
DK = 64
DV = 128
QK_W = HEADS * DK
V_W = HEADS * DV
RANK = 16
GATE_TAU = 16.0
CHUNK = 64
ROPE_BASE = 10000.0
PAST_LEN = 16384
EPS = 1e-6
GN_EPS = 1e-5
A_COLS = 2 * QK_W + 2 * V_W
LANES = 128
WIN_COLS = 2 * A_COLS + QK_W
FOLD_COLS = 256
FOLD_ROWS = 256
SUBLANES = 8
VMEM_LIMIT_BYTES = 58 * 1024 * 1024

PROMPT_TILE = 1024
PROJ_ROWS = 256


def _dot(a, b):
    return jnp.dot(a, b, preferred_element_type=F32)


def _dot_nt(a, b):
    return lax.dot_general(a, b, (((1,), (1,)), ((), ())), preferred_element_type=F32)


def _dot_tn(a, b):
    return lax.dot_general(a, b, (((0,), (0,)), ((), ())), preferred_element_type=F32)


def _rmsnorm(x, w):
    return x * lax.rsqrt(jnp.mean(x * x, axis=-1, keepdims=True) + EPS) * w


def _silu(x):
    return x * (1.0 / (1.0 + jnp.exp(-x)))


def _log_sigmoid(x):
    return jnp.minimum(x, 0.0) - jnp.log1p(jnp.exp(-jnp.abs(x)))


def _ret_gamma(h):
    return 1.0 - 2.0 ** (-5.0 - h)


def _rope(x, cos, sin_signed):
    lane = lax.broadcasted_iota(jnp.int32, (x.shape[0], LANES), 1)
    first_half = (lane % DK) < (DK // 2)
    outs = []
    for j in range(QK_W // LANES):
        xs = x[:, j * LANES:(j + 1) * LANES]
        swapped = jnp.where(first_half,
                            pltpu.roll(xs, LANES - DK // 2, axis=1),
                            pltpu.roll(xs, DK // 2, axis=1))
        outs.append(xs * cos + swapped * sin_signed)
    return jnp.concatenate(outs, axis=1)


def _split_hi_lo(x):
    hi = x.astype(BF16)
    lo = (x - hi.astype(F32)).astype(BF16)
    return hi, lo


def _col_broadcast(row):
    return jnp.broadcast_to(row, (LANES, row.shape[1])).T


def _dup_transpose(k):
    parts = []
    for c in range(k.shape[0] // CHUNK):
        blk = k[c * CHUNK:(c + 1) * CHUNK]
        parts += [blk, blk]
    return jnp.concatenate(parts, axis=0).T.astype(BF16)


def _block_diag_keys_t(kt2, same_head):
    kt4 = jnp.concatenate([kt2, kt2], axis=1)
    return jnp.where(same_head, kt4, jnp.zeros_like(kt4))


def _pair_dot(att, values, q, states):
    z = jnp.zeros_like(values[0])
    outs = []
    for p in range(0, HEADS, 2):
        pair = slice(p * DK, (p + 2) * DK)
        rhs = jnp.concatenate([jnp.concatenate([values[p], z], axis=1),
                               jnp.concatenate([z, values[p + 1]], axis=1),
                               jnp.concatenate([states[p], z], axis=1),
                               jnp.concatenate([z, states[p + 1]], axis=1)], axis=0)
        outs.append(_dot(jnp.concatenate([att[:, pair], q[:, pair]], axis=1), rhs))
    return jnp.concatenate(outs, axis=1)


def _head_blocks(v):
    return [v[:, h * DV:(h + 1) * DV] for h in range(HEADS)]


def _head_norm_gate(rows, oa_ref, ob_ref, ga_ref, gb_ref, gnw_ref, rnw_ref, rnb_ref):
    parts = []
    for j in range(HEADS):
        sl = slice(j * DV, (j + 1) * DV)
        o = oa_ref[rows, sl]
        y = o * lax.rsqrt(jnp.mean(o * o, axis=-1, keepdims=True) + EPS) * gnw_ref[...]
        parts.append((y * ga_ref[rows, sl]).astype(BF16))
    for j in range(HEADS):
        sl = slice(j * DV, (j + 1) * DV)
        o = ob_ref[rows, sl]
        d = o - jnp.mean(o, axis=-1, keepdims=True)
        var = jnp.mean(d * d, axis=-1, keepdims=True)
        y = (d * lax.rsqrt(var + GN_EPS)) * rnw_ref[:, sl] + rnb_ref[:, sl]
        parts.append((y * gb_ref[rows, sl]).astype(BF16))
    return jnp.concatenate(parts, axis=1)


def _project(x, nw_ref, win_ref, gb_ref):
    pres, zs = [], []
    for r in range(0, x.shape[0], PROJ_ROWS):
        hb = _rmsnorm(x[r:r + PROJ_ROWS], nw_ref[...]).astype(BF16)
        pres.append(_dot(hb, win_ref[:, 0:QK_W]))
        zs.append(_dot(hb, win_ref[:, QK_W:]))
    pre = jnp.concatenate(pres, axis=0)
    z = jnp.concatenate(zs, axis=0)
    la = _log_sigmoid(pre + gb_ref[...]) * (1.0 / GATE_TAU)
    g0, r0 = 0, A_COLS
    gq = z[:, g0:g0 + QK_W] * (DK ** -0.5)
    gk = z[:, g0 + QK_W:g0 + 2 * QK_W]
    gv = z[:, g0 + 2 * QK_W:g0 + 2 * QK_W + V_W]
    gg = z[:, g0 + 2 * QK_W + V_W:r0]
    rq = z[:, r0:r0 + QK_W]
    rk = z[:, r0 + QK_W:r0 + 2 * QK_W]
    rv = z[:, r0 + 2 * QK_W:r0 + 2 * QK_W + V_W]
    rg = z[:, r0 + 2 * QK_W + V_W:2 * A_COLS]
    return gq, gk, gv, gg, la, rq, rk, rv, rg


def _chunk_cumsum(x):
    row_in_chunk = lax.broadcasted_iota(jnp.int32, x.shape, 0) % CHUNK
    shift = 1
    while shift < CHUNK:
        x = x + jnp.where(row_in_chunk >= shift, pltpu.roll(x, shift, axis=0), 0.0)
        shift *= 2
    return x


N_PROMPT_SCRATCH = 15
N_SAMPLE_SCRATCH = 11


def _prompt_rows(r0, x, nw_ref, win_ref, gb_ref, gnw_ref, rnw_ref, rnb_ref, wo_ref,
                 cos, sin, qdec, kdec, dmat, fnw_ref, y_ref,
                 qd_s, kd_s, ke_s, va_s, ga_s, qb_s, qbd_s, kb_s, kbe_s, vb_s, gbt_s,
                 oa_s, ob_s, s_gla, s_ret, *, final_norm, side_work):
    n_rows = x.shape[0]
    sub = slice(r0, r0 + n_rows)
    dup = slice(2 * r0, 2 * (r0 + n_rows))
    gq, gk, gv, gg, la, rq, rk, rv, rg = _project(x, nw_ref, win_ref, gb_ref)
    side_work()

    b = _chunk_cumsum(la)
    chunk_decay_rows = []
    b_last = []
    for c in range(n_rows // CHUNK):
        last = b[(c + 1) * CHUNK - 1:(c + 1) * CHUNK]
        chunk_decay_rows.append(jnp.exp(last))
        b_last.append(jnp.broadcast_to(last, (CHUNK, QK_W)))
    bl = jnp.concatenate(b_last, axis=0)
    qd_s[sub, :] = (gq * jnp.exp(b)).astype(BF16)
    kd_s[:, dup] = _dup_transpose(gk * jnp.exp(-b))
    ke_s[sub, :] = (gk * jnp.exp(bl - b)).astype(BF16)
    va_s[sub, :] = gv.astype(BF16)
    vb_s[sub, :] = rv.astype(BF16)

    qb = _rope(rq, cos, sin)
    kb = _rope(rk, cos, sin) * (DK ** -0.5)
    qb_s[sub, :] = qb.astype(BF16)
    qbd_s[sub, :] = (qb * qdec).astype(BF16)
    kb_s[:, dup] = _dup_transpose(kb)
    kbe_s[sub, :] = (kb * kdec).astype(BF16)
    ga_s[sub, :] = _silu(gg)
    gbt_s[sub, :] = _silu(rg)

    tril = dmat > 0.0
    same_head = (lax.broadcasted_iota(jnp.int32, (QK_W, HEADS * CHUNK), 0) // DK
                 == lax.broadcasted_iota(jnp.int32, (QK_W, HEADS * CHUNK), 1) // CHUNK)
    n_chunks = n_rows // CHUNK
    chunk_rows = [slice(r0 + c * CHUNK, r0 + (c + 1) * CHUNK) for c in range(n_chunks)]
    att_gla, att_ret = [], []
    for rows in chunk_rows:
        cols = slice(2 * rows.start, 2 * rows.stop)
        scores = _dot(qd_s[rows, :], _block_diag_keys_t(kd_s[:, cols], same_head))
        att_gla.append(jnp.where(tril, scores, 0.0).astype(BF16))
        scores = _dot(qb_s[rows, :], _block_diag_keys_t(kb_s[:, cols], same_head))
        att_ret.append((scores * dmat).astype(BF16))
    kv_gla, kv_ret = [], []
    for rows in chunk_rows:
        ke, v = ke_s[rows, :], _head_blocks(va_s[rows, :])
        kv_gla.append([_dot_tn(ke[:, h * DK:(h + 1) * DK], v[h]) for h in range(HEADS)])
        kbe, vr = kbe_s[rows, :], _head_blocks(vb_s[rows, :])
        kv_ret.append([_dot_tn(kbe[:, h * DK:(h + 1) * DK], vr[h]) for h in range(HEADS)])
    sin_gla, sin_ret = [], []
    for c in range(n_chunks):
        sin_gla.append([s.astype(BF16) for s in s_gla])
        sin_ret.append([s.astype(BF16) for s in s_ret])
        ecol = _col_broadcast(chunk_decay_rows[c])
        s_gla = [ecol[h * DK:(h + 1) * DK] * s_gla[h] + kv_gla[c][h] for h in range(HEADS)]
        s_ret = [_ret_gamma(h) ** CHUNK * s_ret[h] + kv_ret[c][h] for h in range(HEADS)]
    for c, rows in enumerate(chunk_rows):
        oa_s[rows, :] = _pair_dot(att_gla[c], _head_blocks(va_s[rows, :]),
                                  qd_s[rows, :], sin_gla[c])
        ob_s[rows, :] = _pair_dot(att_ret[c], _head_blocks(vb_s[rows, :]),
                                  qbd_s[rows, :], sin_ret[c])

    mixed = _head_norm_gate(sub, oa_s, ob_s, ga_s, gbt_s, gnw_ref, rnw_ref, rnb_ref)
    y = x + _dot(mixed, wo_ref[...])
    if final_norm:
        y = _rmsnorm(y, fnw_ref[...])
    y_ref[0, sub, :] = y
    return s_gla, s_ret


def _prompt_tile(x_ref, nw_ref, win_ref, gb_ref, gnw_ref, rnw_ref, rnb_ref, wo_ref,
                 cos_ref, sin_ref, qdec_ref, kdec_ref, dmat_ref, fnw_ref, y_ref,
                 qd_s, kd_s, ke_s, va_s, ga_s, qb_s, qbd_s, kb_s, kbe_s, vb_s, gbt_s,
                 oa_s, ob_s, stg_s, str_s, *, tile, final_norm, side_work):
    dmat = dmat_ref[...]
    s_gla = [stg_s[h] for h in range(HEADS)]
    s_ret = [str_s[h] for h in range(HEADS)]
    block_rows = tile
    for r0 in range(0, tile, block_rows):
        sub = slice(r0, r0 + block_rows)
        s_gla, s_ret = _prompt_rows(
            r0, x_ref[0, sub, :], nw_ref, win_ref, gb_ref, gnw_ref, rnw_ref, rnb_ref, wo_ref,
            cos_ref[sub, :], sin_ref[sub, :], qdec_ref[sub, :], kdec_ref[sub, :], dmat, fnw_ref,
            y_ref, qd_s, kd_s, ke_s, va_s, ga_s, qb_s, qbd_s, kb_s, kbe_s, vb_s, gbt_s,
            oa_s, ob_s, s_gla, s_ret, final_norm=final_norm,
            side_work=side_work if r0 == 0 else (lambda: None))
    for h in range(HEADS):
        stg_s[h] = s_gla[h]
        str_s[h] = s_ret[h]


def _sample_project(x_ref, nw_ref, win_ref, gb_ref, cos_ref, sin_ref,
                    qa_s, ka_s, al_s, va_s, ga_s, qb_s, kb_s, vb_s, gbt_s):
    gq, gk, gv, gg, la, rq, rk, rv, rg = _project(x_ref[...], nw_ref, win_ref, gb_ref)
    qa_s[...] = gq
    ka_s[...] = gk
    al_s[...] = jnp.exp(la)
    va_s[...] = gv
    ga_s[...] = _silu(gg)
    qb_s[...] = _rope(rq, cos_ref[...], sin_ref[...])
    kb_s[...] = _rope(rk, cos_ref[...], sin_ref[...]) * (DK ** -0.5)
    vb_s[...] = rv
    gbt_s[...] = _silu(rg)


def _sample_update(step, block, sg_in, sr_in, sg_out, sr_out,
                   qa_s, ka_s, al_s, va_s, qb_s, kb_s, vb_s, oa_s, ob_s):
    rows = pl.ds(pl.multiple_of(step * block, block), block)
    qa, ka, al, va = qa_s[rows, :], ka_s[rows, :], al_s[rows, :], va_s[rows, :]
    qb, kb, vb = qb_s[rows, :], kb_s[rows, :], vb_s[rows, :]
    oa_rows, ob_rows = [], []
    for j in range(block):
        one = slice(j, j + 1)
        qcol = _col_broadcast(qa[one])
        kcol = _col_broadcast(ka[one])
        acol = _col_broadcast(al[one])
        qrc = _col_broadcast(qb[one])
        krc = _col_broadcast(kb[one])
        oa_h, ob_h = [], []
        for h in range(HEADS):
            ks = slice(h * DK, (h + 1) * DK)
            vs = slice(h * DV, (h + 1) * DV)
            s_new = acol[ks] * sg_in[j, h] + kcol[ks] * va[one, vs]
            sg_out[j, h] = s_new
            oa_h.append(jnp.sum(qcol[ks] * s_new, axis=0, keepdims=True))
            r_new = _ret_gamma(h) * sr_in[j, h] + krc[ks] * vb[one, vs]
            sr_out[j, h] = r_new
            ob_h.append(jnp.sum(qrc[ks] * r_new, axis=0, keepdims=True))
        oa_rows.append(jnp.concatenate(oa_h, axis=1))
        ob_rows.append(jnp.concatenate(ob_h, axis=1))
    oa_s[rows, :] = jnp.concatenate(oa_rows, axis=0)
    ob_s[rows, :] = jnp.concatenate(ob_rows, axis=0)


def _layer_kernel(*refs, tile, block, final_norm, chained):
    (x_ref, nw_ref, win_ref, gb_ref, gnw_ref, rnw_ref, rnb_ref, wo_ref,
     cos_ref, sin_ref, qdec_ref, kdec_ref, dmat_ref, fnw_ref,
     xs_ref, cos_s_ref, sin_s_ref, sg_in, sr_in) = refs[:19]
    refs = refs[19 + (2 if chained else 0):]
    y_ref, sg_ref, sr_ref, ys_ref, sg_out, sr_out = refs[:6]
    prompt_scratch = refs[6:6 + N_PROMPT_SCRATCH]
    (qa_s, ka_s, al_s, vsa_s, gsa_s, qsb_s, ksb_s, vsb_s, gsb_s, osa_s, osb_s) = \
        refs[6 + N_PROMPT_SCRATCH:]
    stg_s, str_s = prompt_scratch[-2:]
    t = pl.program_id(1)
    step = pl.program_id(0) * pl.num_programs(1) + t
    last_step = pl.num_programs(0) * pl.num_programs(1) - 1

    @pl.when(t == 0)
    def _zero_prompt_states():
        stg_s[...] = jnp.zeros_like(stg_s)
        str_s[...] = jnp.zeros_like(str_s)

    @pl.when(step == 0)
    def _project_decode_rows():
        _sample_project(xs_ref, nw_ref, win_ref, gb_ref, cos_s_ref, sin_s_ref,
                        qa_s, ka_s, al_s, vsa_s, gsa_s, qsb_s, ksb_s, vsb_s, gsb_s)

    decode_rows = functools.partial(
        _sample_update, step, block, sg_in, sr_in, sg_out, sr_out,
        qa_s, ka_s, al_s, vsa_s, qsb_s, ksb_s, vsb_s, osa_s, osb_s)
    _prompt_tile(x_ref, nw_ref, win_ref, gb_ref, gnw_ref, rnw_ref, rnb_ref, wo_ref,
                 cos_ref, sin_ref, qdec_ref, kdec_ref, dmat_ref, fnw_ref, y_ref,
                 *prompt_scratch, tile=tile, final_norm=final_norm, side_work=decode_rows)

    @pl.when(t == pl.num_programs(1) - 1)
    def _emit_prompt_states():
        sg_ref[0] = stg_s[...]
        sr_ref[0] = str_s[...]

    @pl.when(step == last_step)
    def _finish_decode_rows():
        mixed = _head_norm_gate(slice(None), osa_s, osb_s, gsa_s, gsb_s, gnw_ref, rnw_ref,
                                rnb_ref)
        ys = xs_ref[...] + _dot(mixed, wo_ref[...])
        if final_norm:
            ys = _rmsnorm(ys, fnw_ref[...])
        ys_ref[...] = ys


def _const_spec(shape):
    nd = len(shape)
    return pl.BlockSpec(shape, lambda *_: (0,) * nd)


def _layer_spec(shape, layer):
    nd = len(shape)
    return pl.BlockSpec((None,) + tuple(shape[1:]), lambda *_: (layer,) + (0,) * (nd - 1))


def _layer(x, xs, wts, tables, sample_tables, fnw, sg, sr, prev_out, *, layer, final_norm):
    bsz, seq, d = x.shape
    n = xs.shape[0]
    tile = min(PROMPT_TILE, seq)
    assert seq % tile == 0
    nt = seq // tile
    steps = bsz * nt
    block = n // steps
    assert block * steps == n and block % SUBLANES == 0, "decode rows must split evenly over steps"
    cos, sin, qdec, kdec, dmat = tables
    cos_s, sin_s = sample_tables
    tile_spec = pl.BlockSpec((1, tile, d), lambda b, t: (b, t, 0))
    in_specs = [tile_spec] + [_layer_spec(a.shape, layer) for a in wts]
    in_specs += [pl.BlockSpec((tile, LANES), lambda b, t: (t, 0)),
                 pl.BlockSpec((tile, LANES), lambda b, t: (t, 0)),
                 _const_spec(qdec.shape), _const_spec(kdec.shape), _const_spec(dmat.shape),
                 _const_spec(fnw.shape), _const_spec(xs.shape), _const_spec(cos_s.shape),
                 _const_spec(sin_s.shape)]
    st_spec = pl.BlockSpec((None, block, HEADS, DK, DV), lambda b, t: (layer, b * nt + t, 0, 0, 0))
    in_specs += [st_spec, st_spec]
    operands = [x, *wts, cos, sin, qdec, kdec, dmat, fnw, xs, cos_s, sin_s, sg, sr]
    aliases = {}
    if prev_out is not None:
        in_specs += [pl.BlockSpec(memory_space=pl.ANY)] * 2
        aliases = {len(operands): 4, len(operands) + 1: 5}
        operands += list(prev_out)
    pstate_shape = jax.ShapeDtypeStruct((bsz, HEADS, DK, DV), F32)
    pstate_spec = pl.BlockSpec((1, HEADS, DK, DV), lambda b, t: (b, 0, 0, 0))
    prompt_scratch = (
        [pltpu.VMEM((tile, QK_W), BF16)]
        + [pltpu.VMEM((QK_W, 2 * tile), BF16)]
        + [pltpu.VMEM((tile, QK_W), BF16)]
        + [pltpu.VMEM((tile, V_W), BF16)]
        + [pltpu.VMEM((tile, V_W), F32)]
        + [pltpu.VMEM((tile, QK_W), BF16)] * 2
        + [pltpu.VMEM((QK_W, 2 * tile), BF16)]
        + [pltpu.VMEM((tile, QK_W), BF16)]
        + [pltpu.VMEM((tile, V_W), BF16)]
        + [pltpu.VMEM((tile, V_W), F32)] * 3
        + [pltpu.VMEM((HEADS, DK, DV), F32)] * 2
    )
    sample_scratch = (
        [pltpu.VMEM((n, QK_W), F32)] * 3
        + [pltpu.VMEM((n, V_W), F32)] * 2
        + [pltpu.VMEM((n, QK_W), F32)] * 2
        + [pltpu.VMEM((n, V_W), F32)] * 4
    )
    assert len(prompt_scratch) == N_PROMPT_SCRATCH and len(sample_scratch) == N_SAMPLE_SCRATCH
    y, sgp, srp, ys, sg_new, sr_new = pl.pallas_call(
        functools.partial(_layer_kernel, tile=tile, block=block, final_norm=final_norm,
                          chained=prev_out is not None),
        grid=(bsz, nt),
        in_specs=in_specs,
        out_specs=[tile_spec, pstate_spec, pstate_spec, _const_spec(xs.shape), st_spec, st_spec],
        out_shape=[jax.ShapeDtypeStruct(x.shape, F32), pstate_shape, pstate_shape,
                   jax.ShapeDtypeStruct(xs.shape, F32),
                   jax.ShapeDtypeStruct(sg.shape, F32), jax.ShapeDtypeStruct(sr.shape, F32)],
        input_output_aliases=aliases,
        scratch_shapes=prompt_scratch + sample_scratch,
        compiler_params=pltpu.CompilerParams(
            dimension_semantics=("arbitrary", "arbitrary"),
            vmem_limit_bytes=VMEM_LIMIT_BYTES),
        name="hybrid_layer",
    )(*operands)
    return y, sgp, srp, ys, (sg_new, sr_new)


def _rope_tables(pos):
    half = DK // 2
    inv = ROPE_BASE ** (-np.arange(half, dtype=np.float64) / half)
    ang = np.asarray(pos, dtype=np.float64)[:, None] * inv[None, :]
    cos = np.tile(np.cos(ang), (1, LANES // half))
    sin = np.tile(np.concatenate([-np.sin(ang), np.sin(ang)], axis=1), (1, LANES // DK))
    return jnp.asarray(cos, F32), jnp.asarray(sin, F32)


def _retention_tables(tile):
    idx = np.arange(CHUNK, dtype=np.float64)
    gam = np.array([_ret_gamma(h) for h in range(HEADS)], dtype=np.float64)
    qdec = np.repeat(gam[None, :] ** (idx[:, None] + 1.0), DK, axis=1)
    kdec = np.repeat(gam[None, :] ** (CHUNK - 1.0 - idx[:, None]), DK, axis=1)
    rel = idx[:, None] - idx[None, :]
    dm = [np.where(rel >= 0, gam[h] ** np.maximum(rel, 0.0), 0.0) for h in range(HEADS)]
    dmat = np.concatenate(dm, axis=1)
    reps = tile // CHUNK
    return (jnp.asarray(np.tile(qdec, (reps, 1)), F32), jnp.asarray(np.tile(kdec, (reps, 1)), F32),
            jnp.asarray(dmat, F32))


def _fold_weights_kernel(wt_ref, w2_ref, wout_ref, fused_ref, wout_bf_ref):
    a_hi, a_lo = _split_hi_lo(wt_ref[A_COLS:A_COLS + RANK, :])
    b_hi, b_lo = _split_hi_lo(w2_ref[...])
    folded = _dot_tn(a_hi, b_hi) + _dot_tn(a_hi, b_lo) + _dot_tn(a_lo, b_hi)
    fused_ref[:, 0:QK_W] = folded.astype(BF16)
    for j in range(1, WIN_COLS // FOLD_COLS):
        start = (j - 1) * FOLD_COLS + (RANK if j > A_COLS // FOLD_COLS else 0)
        fused_ref[:, j * FOLD_COLS:(j + 1) * FOLD_COLS] = \
            wt_ref[start:start + FOLD_COLS, :].astype(BF16).T
    wout_bf_ref[...] = wout_ref[...].astype(BF16)


def _fold_weights(w_in, gla_w2, w_out):
    depth, d, in_cols = w_in.shape
    assert in_cols == 2 * A_COLS + RANK and A_COLS % FOLD_COLS == 0 and QK_W == FOLD_COLS
    assert d % FOLD_ROWS == 0
    w_in_t = jnp.swapaxes(w_in, 1, 2)
    d_out = w_out.shape[2]
    return pl.pallas_call(
        _fold_weights_kernel,
        grid=(depth, d // FOLD_ROWS),
        in_specs=[pl.BlockSpec((None, in_cols, FOLD_ROWS), lambda l, i: (l, 0, i)),
                  pl.BlockSpec((None, RANK, QK_W), lambda l, i: (l, 0, 0)),
                  pl.BlockSpec((None, FOLD_ROWS, d_out), lambda l, i: (l, i, 0))],
        out_specs=[pl.BlockSpec((None, FOLD_ROWS, WIN_COLS), lambda l, i: (l, i, 0)),
                   pl.BlockSpec((None, FOLD_ROWS, d_out), lambda l, i: (l, i, 0))],
        out_shape=[jax.ShapeDtypeStruct((depth, d, WIN_COLS), BF16),
                   jax.ShapeDtypeStruct(w_out.shape, BF16)],
        compiler_params=pltpu.CompilerParams(dimension_semantics=("arbitrary", "arbitrary")),
        name="fold_weights",
    )(w_in_t, gla_w2, w_out)


def _stacked_params(fused_in, wout_bf, norm_w, gla_b, gla_norm_w, ret_norm_w, ret_norm_b):
    row = lambda p: p[:, None, :]
    return (row(norm_w), fused_in, row(gla_b), row(gla_norm_w), row(ret_norm_w), row(ret_norm_b),
            wout_bf)


def kernel(x_prompt, x_sample, state_gla, state_ret, norm_w, w_in, gla_w2, gla_b, gla_norm_w,
           ret_norm_w, ret_norm_b, w_out, final_norm_w):
    bp, tp, d = x_prompt.shape
    bs, ts, _ = x_sample.shape
    assert ts == 1, "the decode path handles one new token per sequence"
    depth = w_in.shape[0]
    tile = min(PROMPT_TILE, tp)
    prompt_tables = _rope_tables(np.arange(tp)) + _retention_tables(tile)
    sample_tables = _rope_tables(PAST_LEN + np.arange(ts))
    fnw = final_norm_w[None, :]

    hp = x_prompt
    hs = x_sample.reshape(bs, d)
    gla_p, ret_p = [], []
    sample_states = None
    fused_in, wout_bf = _fold_weights(w_in, gla_w2, w_out)
    wts = _stacked_params(fused_in, wout_bf, norm_w, gla_b, gla_norm_w, ret_norm_w, ret_norm_b)
    for l in range(depth):
        hp, sg, sr, hs, sample_states = _layer(
            hp, hs, wts, prompt_tables, sample_tables, fnw, state_gla, state_ret, sample_states,
            layer=l, final_norm=l == depth - 1)
        gla_p.append(sg)
        ret_p.append(sr)
    return (hp, hs.reshape(bs, ts, d), jnp.stack(gla_p), jnp.stack(ret_p),
            sample_states[0], sample_states[1])
```

```python
import functools

import numpy as np
import jax
import jax.numpy as jnp
from jax import lax
from jax.experimental import pallas as pl
from jax.experimental.pallas import tpu as pltpu

F32 = jnp.float32
BF16 = jnp.bfloat16

HEADS = 4
DK = 64
DV = 128
QK_W = HEADS * DK
V_W = HEADS * DV
RANK = 16
GATE_TAU = 16.0
CHUNK = 64
ROPE_BASE = 10000.0
PAST_LEN = 16384
EPS = 1e-6
GN_EPS = 1e-5
A_COLS = 2 * QK_W + 2 * V_W
LANES = 128
WIN_COLS = 2 * A_COLS + QK_W
FOLD_COLS = 256
FOLD_ROWS = 256
SUBLANES = 8
VMEM_LIMIT_BYTES = 58 * 1024 * 1024

PROMPT_TILE = 1024
PROJ_ROWS = 512


def _dot(a, b):
    return jnp.dot(a, b, preferred_element_type=F32)


def _dot_nt(a, b):
    return lax.dot_general(a, b, (((1,), (1,)), ((), ())), preferred_element_type=F32)


def _dot_tn(a, b):
    return lax.dot_general(a, b, (((0,), (0,)), ((), ())), preferred_element_type=F32)


def _rmsnorm(x, w):
    return x * lax.rsqrt(jnp.mean(x * x, axis=-1, keepdims=True) + EPS) * w


def _silu(x):
    return x * (1.0 / (1.0 + jnp.exp(-x)))


def _log_sigmoid(x):
    return jnp.minimum(x, 0.0) - jnp.log1p(jnp.exp(-jnp.abs(x)))


def _ret_gamma(h):
    return 1.0 - 2.0 ** (-5.0 - h)


def _rope(x, cos, sin_signed):
    lane = lax.broadcasted_iota(jnp.int32, (x.shape[0], LANES), 1)
    first_half = (lane % DK) < (DK // 2)
    outs = []
    for j in range(QK_W // LANES):
        xs = x[:, j * LANES:(j + 1) * LANES]
        swapped = jnp.where(first_half,
                            pltpu.roll(xs, LANES - DK // 2, axis=1),
                            pltpu.roll(xs, DK // 2, axis=1))
        outs.append(xs * cos + swapped * sin_signed)
    return jnp.concatenate(outs, axis=1)


def _split_hi_lo(x):
    hi = x.astype(BF16)
    lo = (x - hi.astype(F32)).astype(BF16)
    return hi, lo


def _col_broadcast(row):
    return jnp.broadcast_to(row, (LANES, row.shape[1])).T


def _dup_transpose(k):
    parts = []
    for c in range(k.shape[0] // CHUNK):
        blk = k[c * CHUNK:(c + 1) * CHUNK]
        parts += [blk, blk]
    return jnp.concatenate(parts, axis=0).T.astype(BF16)


def _block_diag_keys_t(kt2, same_head):
    kt4 = jnp.concatenate([kt2, kt2], axis=1)
    return jnp.where(same_head, kt4, jnp.zeros_like(kt4))


def _pair_dot(lhs, blocks):
    z = jnp.zeros_like(blocks[0])
    outs = []
    for p in range(0, HEADS, 2):
        rhs = jnp.concatenate([jnp.concatenate([blocks[p], z], axis=1),
                               jnp.concatenate([z, blocks[p + 1]], axis=1)], axis=0)
        outs.append(_dot(lhs[:, p * DK:(p + 2) * DK], rhs))
    return jnp.concatenate(outs, axis=1)


def _head_blocks(v):
    return [v[:, h * DV:(h + 1) * DV] for h in range(HEADS)]


def _head_norm_gate(rows, oa_ref, ob_ref, ga_ref, gb_ref, gnw_ref, rnw_ref, rnb_ref):
    parts = []
    for j in range(HEADS):
        sl = slice(j * DV, (j + 1) * DV)
        o = oa_ref[rows, sl]
        y = o * lax.rsqrt(jnp.mean(o * o, axis=-1, keepdims=True) + EPS) * gnw_ref[...]
        parts.append((y * ga_ref[rows, sl]).astype(BF16))
    for j in range(HEADS):
        sl = slice(j * DV, (j + 1) * DV)
        o = ob_ref[rows, sl]
        d = o - jnp.mean(o, axis=-1, keepdims=True)
        var = jnp.mean(d * d, axis=-1, keepdims=True)
        y = (d * lax.rsqrt(var + GN_EPS)) * rnw_ref[:, sl] + rnb_ref[:, sl]
        parts.append((y * gb_ref[rows, sl]).astype(BF16))
    return jnp.concatenate(parts, axis=1)


def _project(x, nw_ref, win_ref, gb_ref):
    pres, zs = [], []
    for r in range(0, x.shape[0], PROJ_ROWS):
        hb = _rmsnorm(x[r:r + PROJ_ROWS], nw_ref[...]).astype(BF16)
        pres.append(_dot(hb, win_ref[:, 0:QK_W]))
        zs.append(_dot(hb, win_ref[:, QK_W:]))
    pre = jnp.concatenate(pres, axis=0)
    z = jnp.concatenate(zs, axis=0)
    la = _log_sigmoid(pre + gb_ref[...]) * (1.0 / GATE_TAU)
    g0, r0 = 0, A_COLS
    gq = z[:, g0:g0 + QK_W] * (DK ** -0.5)
    gk = z[:, g0 + QK_W:g0 + 2 * QK_W]
    gv = z[:, g0 + 2 * QK_W:g0 + 2 * QK_W + V_W]
    gg = z[:, g0 + 2 * QK_W + V_W:r0]
    rq = z[:, r0:r0 + QK_W]
    rk = z[:, r0 + QK_W:r0 + 2 * QK_W]
    rv = z[:, r0 + 2 * QK_W:r0 + 2 * QK_W + V_W]
    rg = z[:, r0 + 2 * QK_W + V_W:2 * A_COLS]
    return gq, gk, gv, gg, la, rq, rk, rv, rg


def _chunk_cumsum(x):
    row_in_chunk = lax.broadcasted_iota(jnp.int32, x.shape, 0) % CHUNK
    shift = 1
    while shift < CHUNK:
        x = x + jnp.where(row_in_chunk >= shift, pltpu.roll(x, shift, axis=0), 0.0)
        shift *= 2
    return x


N_PROMPT_SCRATCH = 15
N_SAMPLE_SCRATCH = 11


def _prompt_rows(r0, x, nw_ref, win_ref, gb_ref, gnw_ref, rnw_ref, rnb_ref, wo_ref,
                 cos, sin, qdec, kdec, dmat, fnw_ref, y_ref,
                 qd_s, kd_s, ke_s, va_s, ga_s, qb_s, qbd_s, kb_s, kbe_s, vb_s, gbt_s,
                 oa_s, ob_s, s_gla, s_ret, *, final_norm, side_work):
    n_rows = x.shape[0]
    sub = slice(r0, r0 + n_rows)
    dup = slice(2 * r0, 2 * (r0 + n_rows))
    gq, gk, gv, gg, la, rq, rk, rv, rg = _project(x, nw_ref, win_ref, gb_ref)
    side_work()

    b = _chunk_cumsum(la)
    chunk_decay_rows = []
    b_last = []
    for c in range(n_rows // CHUNK):
        last = b[(c + 1) * CHUNK - 1:(c + 1) * CHUNK]
        chunk_decay_rows.append(jnp.exp(last))
        b_last.append(jnp.broadcast_to(last, (CHUNK, QK_W)))
    bl = jnp.concatenate(b_last, axis=0)
    qd_s[sub, :] = (gq * jnp.exp(b)).astype(BF16)
    kd_s[:, dup] = _dup_transpose(gk * jnp.exp(-b))
    ke_s[sub, :] = (gk * jnp.exp(bl - b)).astype(BF16)
    va_s[sub, :] = gv.astype(BF16)
    vb_s[sub, :] = rv.astype(BF16)

    qb = _rope(rq, cos, sin)
    kb = _rope(rk, cos, sin) * (DK ** -0.5)
    qb_s[sub, :] = qb.astype(BF16)
    qbd_s[sub, :] = (qb * qdec).astype(BF16)
    kb_s[:, dup] = _dup_transpose(kb)
    kbe_s[sub, :] = (kb * kdec).astype(BF16)
    ga_s[sub, :] = _silu(gg)
    gbt_s[sub, :] = _silu(rg)

    tril = dmat > 0.0
    same_head = (lax.broadcasted_iota(jnp.int32, (QK_W, HEADS * CHUNK), 0) // DK
                 == lax.broadcasted_iota(jnp.int32, (QK_W, HEADS * CHUNK), 1) // CHUNK)
    n_chunks = n_rows // CHUNK
    chunk_rows = [slice(r0 + c * CHUNK, r0 + (c + 1) * CHUNK) for c in range(n_chunks)]
    att_gla, att_ret = [], []
    for rows in chunk_rows:
        cols = slice(2 * rows.start, 2 * rows.stop)
        scores = _dot(qd_s[rows, :], _block_diag_keys_t(kd_s[:, cols], same_head))
        att_gla.append(jnp.where(tril, scores, 0.0).astype(BF16))
        scores = _dot(qb_s[rows, :], _block_diag_keys_t(kb_s[:, cols], same_head))
        att_ret.append((scores * dmat).astype(BF16))
    kv_gla, kv_ret = [], []
    for rows in chunk_rows:
        ke, v = ke_s[rows, :], _head_blocks(va_s[rows, :])
        kv_gla.append([_dot_tn(ke[:, h * DK:(h + 1) * DK], v[h]) for h in range(HEADS)])
        kbe, vr = kbe_s[rows, :], _head_blocks(vb_s[rows, :])
        kv_ret.append([_dot_tn(kbe[:, h * DK:(h + 1) * DK], vr[h]) for h in range(HEADS)])
    sin_gla, sin_ret = [], []
    for c in range(n_chunks):
        sin_gla.append([s.astype(BF16) for s in s_gla])
        sin_ret.append([s.astype(BF16) for s in s_ret])
        ecol = _col_broadcast(chunk_decay_rows[c])
        s_gla = [ecol[h * DK:(h + 1) * DK] * s_gla[h] + kv_gla[c][h] for h in range(HEADS)]
        s_ret = [_ret_gamma(h) ** CHUNK * s_ret[h] + kv_ret[c][h] for h in range(HEADS)]
    for c, rows in enumerate(chunk_rows):
        oa_s[rows, :] = (_pair_dot(att_gla[c], _head_blocks(va_s[rows, :]))
                         + _pair_dot(qd_s[rows, :], sin_gla[c]))
        ob_s[rows, :] = (_pair_dot(att_ret[c], _head_blocks(vb_s[rows, :]))
                         + _pair_dot(qbd_s[rows, :], sin_ret[c]))

    mixed = _head_norm_gate(sub, oa_s, ob_s, ga_s, gbt_s, gnw_ref, rnw_ref, rnb_ref)
    y = x + _dot(mixed, wo_ref[...])
    if final_norm:
        y = _rmsnorm(y, fnw_ref[...])
    y_ref[0, sub, :] = y
    return s_gla, s_ret


def _prompt_tile(x_ref, nw_ref, win_ref, gb_ref, gnw_ref, rnw_ref, rnb_ref, wo_ref,
                 cos_ref, sin_ref, qdec_ref, kdec_ref, dmat_ref, fnw_ref, y_ref,
                 qd_s, kd_s, ke_s, va_s, ga_s, qb_s, qbd_s, kb_s, kbe_s, vb_s, gbt_s,
                 oa_s, ob_s, stg_s, str_s, *, tile, final_norm, side_work):
    dmat = dmat_ref[...]
    s_gla = [stg_s[h] for h in range(HEADS)]
    s_ret = [str_s[h] for h in range(HEADS)]
    block_rows = tile
    for r0 in range(0, tile, block_rows):
        sub = slice(r0, r0 + block_rows)
        s_gla, s_ret = _prompt_rows(
            r0, x_ref[0, sub, :], nw_ref, win_ref, gb_ref, gnw_ref, rnw_ref, rnb_ref, wo_ref,
            cos_ref[sub, :], sin_ref[sub, :], qdec_ref[sub, :], kdec_ref[sub, :], dmat, fnw_ref,
            y_ref, qd_s, kd_s, ke_s, va_s, ga_s, qb_s, qbd_s, kb_s, kbe_s, vb_s, gbt_s,
            oa_s, ob_s, s_gla, s_ret, final_norm=final_norm,
            side_work=side_work if r0 == 0 else (lambda: None))
    for h in range(HEADS):
        stg_s[h] = s_gla[h]
        str_s[h] = s_ret[h]


def _sample_project(x_ref, nw_ref, win_ref, gb_ref, cos_ref, sin_ref,
                    qa_s, ka_s, al_s, va_s, ga_s, qb_s, kb_s, vb_s, gbt_s):
    gq, gk, gv, gg, la, rq, rk, rv, rg = _project(x_ref[...], nw_ref, win_ref, gb_ref)
    qa_s[...] = gq
    ka_s[...] = gk
    al_s[...] = jnp.exp(la)
    va_s[...] = gv
    ga_s[...] = _silu(gg)
    qb_s[...] = _rope(rq, cos_ref[...], sin_ref[...])
    kb_s[...] = _rope(rk, cos_ref[...], sin_ref[...]) * (DK ** -0.5)
    vb_s[...] = rv
    gbt_s[...] = _silu(rg)


def _sample_update(step, block, sg_in, sr_in, sg_out, sr_out,
                   qa_s, ka_s, al_s, va_s, qb_s, kb_s, vb_s, oa_s, ob_s):
    rows = pl.ds(pl.multiple_of(step * block, block), block)
    qa, ka, al, va = qa_s[rows, :], ka_s[rows, :], al_s[rows, :], va_s[rows, :]
    qb, kb, vb = qb_s[rows, :], kb_s[rows, :], vb_s[rows, :]
    oa_rows, ob_rows = [], []
    for j in range(block):
        one = slice(j, j + 1)
        qcol = _col_broadcast(qa[one])
        kcol = _col_broadcast(ka[one])
        acol = _col_broadcast(al[one])
        qrc = _col_broadcast(qb[one])
        krc = _col_broadcast(kb[one])
        oa_h, ob_h = [], []
        for h in range(HEADS):
            ks = slice(h * DK, (h + 1) * DK)
            vs = slice(h * DV, (h + 1) * DV)
            s_new = acol[ks] * sg_in[j, h] + kcol[ks] * va[one, vs]
            sg_out[j, h] = s_new
            oa_h.append(jnp.sum(qcol[ks] * s_new, axis=0, keepdims=True))
            r_new = _ret_gamma(h) * sr_in[j, h] + krc[ks] * vb[one, vs]
            sr_out[j, h] = r_new
            ob_h.append(jnp.sum(qrc[ks] * r_new, axis=0, keepdims=True))
        oa_rows.append(jnp.concatenate(oa_h, axis=1))
        ob_rows.append(jnp.concatenate(ob_h, axis=1))
    oa_s[rows, :] = jnp.concatenate(oa_rows, axis=0)
    ob_s[rows, :] = jnp.concatenate(ob_rows, axis=0)


def _layer_kernel(*refs, tile, block, final_norm, chained):
    (x_ref, nw_ref, win_ref, gb_ref, gnw_ref, rnw_ref, rnb_ref, wo_ref,
     cos_ref, sin_ref, qdec_ref, kdec_ref, dmat_ref, fnw_ref,
     xs_ref, cos_s_ref, sin_s_ref, sg_in, sr_in) = refs[:19]
    refs = refs[19 + (2 if chained else 0):]
    y_ref, sg_ref, sr_ref, ys_ref, sg_out, sr_out = refs[:6]
    prompt_scratch = refs[6:6 + N_PROMPT_SCRATCH]
    (qa_s, ka_s, al_s, vsa_s, gsa_s, qsb_s, ksb_s, vsb_s, gsb_s, osa_s, osb_s) = \
        refs[6 + N_PROMPT_SCRATCH:]
    stg_s, str_s = prompt_scratch[-2:]
    t = pl.program_id(1)
    step = pl.program_id(0) * pl.num_programs(1) + t
    last_step = pl.num_programs(0) * pl.num_programs(1) - 1

    @pl.when(t == 0)
    def _zero_prompt_states():
        stg_s[...] = jnp.zeros_like(stg_s)
        str_s[...] = jnp.zeros_like(str_s)

    @pl.when(step == 0)
    def _project_decode_rows():
        _sample_project(xs_ref, nw_ref, win_ref, gb_ref, cos_s_ref, sin_s_ref,
                        qa_s, ka_s, al_s, vsa_s, gsa_s, qsb_s, ksb_s, vsb_s, gsb_s)

    decode_rows = functools.partial(
        _sample_update, step, block, sg_in, sr_in, sg_out, sr_out,
        qa_s, ka_s, al_s, vsa_s, qsb_s, ksb_s, vsb_s, osa_s, osb_s)
    _prompt_tile(x_ref, nw_ref, win_ref, gb_ref, gnw_ref, rnw_ref, rnb_ref, wo_ref,
                 cos_ref, sin_ref, qdec_ref, kdec_ref, dmat_ref, fnw_ref, y_ref,
                 *prompt_scratch, tile=tile, final_norm=final_norm, side_work=decode_rows)

    @pl.when(t == pl.num_programs(1) - 1)
    def _emit_prompt_states():
        sg_ref[0] = stg_s[...]
        sr_ref[0] = str_s[...]

    @pl.when(step == last_step)
    def _finish_decode_rows():
        mixed = _head_norm_gate(slice(None), osa_s, osb_s, gsa_s, gsb_s, gnw_ref, rnw_ref,
                                rnb_ref)
        ys = xs_ref[...] + _dot(mixed, wo_ref[...])
        if final_norm:
            ys = _rmsnorm(ys, fnw_ref[...])
        ys_ref[...] = ys


def _const_spec(shape):
    nd = len(shape)
    return pl.BlockSpec(shape, lambda *_: (0,) * nd)


def _layer_spec(shape, layer):
    nd = len(shape)
    return pl.BlockSpec((None,) + tuple(shape[1:]), lambda *_: (layer,) + (0,) * (nd - 1))


def _layer(x, xs, wts, tables, sample_tables, fnw, sg, sr, prev_out, *, layer, final_norm):
    bsz, seq, d = x.shape
    n = xs.shape[0]
    tile = min(PROMPT_TILE, seq)
    assert seq % tile == 0
    nt = seq // tile
    steps = bsz * nt
    block = n // steps
    assert block * steps == n and block % SUBLANES == 0, "decode rows must split evenly over steps"
    cos, sin, qdec, kdec, dmat = tables
    cos_s, sin_s = sample_tables
    tile_spec = pl.BlockSpec((1, tile, d), lambda b, t: (b, t, 0))
    in_specs = [tile_spec] + [_layer_spec(a.shape, layer) for a in wts]
    in_specs += [pl.BlockSpec((tile, LANES), lambda b, t: (t, 0)),
                 pl.BlockSpec((tile, LANES), lambda b, t: (t, 0)),
                 _const_spec(qdec.shape), _const_spec(kdec.shape), _const_spec(dmat.shape),
                 _const_spec(fnw.shape), _const_spec(xs.shape), _const_spec(cos_s.shape),
                 _const_spec(sin_s.shape)]
    st_spec = pl.BlockSpec((None, block, HEADS, DK, DV), lambda b, t: (layer, b * nt + t, 0, 0, 0))
    in_specs += [st_spec, st_spec]
    operands = [x, *wts, cos, sin, qdec, kdec, dmat, fnw, xs, cos_s, sin_s, sg, sr]
    aliases = {}
    if prev_out is not None:
        in_specs += [pl.BlockSpec(memory_space=pl.ANY)] * 2
        aliases = {len(operands): 4, len(operands) + 1: 5}
        operands += list(prev_out)
    pstate_shape = jax.ShapeDtypeStruct((bsz, HEADS, DK, DV), F32)
    pstate_spec = pl.BlockSpec((1, HEADS, DK, DV), lambda b, t: (b, 0, 0, 0))
    prompt_scratch = (
        [pltpu.VMEM((tile, QK_W), BF16)]
        + [pltpu.VMEM((QK_W, 2 * tile), BF16)]
        + [pltpu.VMEM((tile, QK_W), BF16)]
        + [pltpu.VMEM((tile, V_W), BF16)]
        + [pltpu.VMEM((tile, V_W), F32)]
        + [pltpu.VMEM((tile, QK_W), BF16)] * 2
        + [pltpu.VMEM((QK_W, 2 * tile), BF16)]
        + [pltpu.VMEM((tile, QK_W), BF16)]
        + [pltpu.VMEM((tile, V_W), BF16)]
        + [pltpu.VMEM((tile, V_W), F32)] * 3
        + [pltpu.VMEM((HEADS, DK, DV), F32)] * 2
    )
    sample_scratch = (
        [pltpu.VMEM((n, QK_W), F32)] * 3
        + [pltpu.VMEM((n, V_W), F32)] * 2
        + [pltpu.VMEM((n, QK_W), F32)] * 2
        + [pltpu.VMEM((n, V_W), F32)] * 4
    )
    assert len(prompt_scratch) == N_PROMPT_SCRATCH and len(sample_scratch) == N_SAMPLE_SCRATCH
    y, sgp, srp, ys, sg_new, sr_new = pl.pallas_call(
        functools.partial(_layer_kernel, tile=tile, block=block, final_norm=final_norm,
                          chained=prev_out is not None),
        grid=(bsz, nt),
        in_specs=in_specs,
        out_specs=[tile_spec, pstate_spec, pstate_spec, _const_spec(xs.shape), st_spec, st_spec],
        out_shape=[jax.ShapeDtypeStruct(x.shape, F32), pstate_shape, pstate_shape,
                   jax.ShapeDtypeStruct(xs.shape, F32),
                   jax.ShapeDtypeStruct(sg.shape, F32), jax.ShapeDtypeStruct(sr.shape, F32)],
        input_output_aliases=aliases,
        scratch_shapes=prompt_scratch + sample_scratch,
        compiler_params=pltpu.CompilerParams(
            dimension_semantics=("arbitrary", "arbitrary"),
            vmem_limit_bytes=VMEM_LIMIT_BYTES),
        name="hybrid_layer",
    )(*operands)
    return y, sgp, srp, ys, (sg_new, sr_new)


def _rope_tables(pos):
    half = DK // 2
    inv = ROPE_BASE ** (-np.arange(half, dtype=np.float64) / half)
    ang = np.asarray(pos, dtype=np.float64)[:, None] * inv[None, :]
    cos = np.tile(np.cos(ang), (1, LANES // half))
    sin = np.tile(np.concatenate([-np.sin(ang), np.sin(ang)], axis=1), (1, LANES // DK))
    return jnp.asarray(cos, F32), jnp.asarray(sin, F32)


def _retention_tables(tile):
    idx = np.arange(CHUNK, dtype=np.float64)
    gam = np.array([_ret_gamma(h) for h in range(HEADS)], dtype=np.float64)
    qdec = np.repeat(gam[None, :] ** (idx[:, None] + 1.0), DK, axis=1)
    kdec = np.repeat(gam[None, :] ** (CHUNK - 1.0 - idx[:, None]), DK, axis=1)
    rel = idx[:, None] - idx[None, :]
    dm = [np.where(rel >= 0, gam[h] ** np.maximum(rel, 0.0), 0.0) for h in range(HEADS)]
    dmat = np.concatenate(dm, axis=1)
    reps = tile // CHUNK
    return (jnp.asarray(np.tile(qdec, (reps, 1)), F32), jnp.asarray(np.tile(kdec, (reps, 1)), F32),
            jnp.asarray(dmat, F32))


def _fold_weights_kernel(wt_ref, w2_ref, wout_ref, fused_ref, wout_bf_ref):
    a_hi, a_lo = _split_hi_lo(wt_ref[A_COLS:A_COLS + RANK, :])
    b_hi, b_lo = _split_hi_lo(w2_ref[...])
    folded = _dot_tn(a_hi, b_hi) + _dot_tn(a_hi, b_lo) + _dot_tn(a_lo, b_hi)
    fused_ref[:, 0:QK_W] = folded.astype(BF16)
    for j in range(1, WIN_COLS // FOLD_COLS):
        start = (j - 1) * FOLD_COLS + (RANK if j > A_COLS // FOLD_COLS else 0)
        fused_ref[:, j * FOLD_COLS:(j + 1) * FOLD_COLS] = \
            wt_ref[start:start + FOLD_COLS, :].astype(BF16).T
    wout_bf_ref[...] = wout_ref[...].astype(BF16)


def _fold_weights(w_in, gla_w2, w_out):
    depth, d, in_cols = w_in.shape
    assert in_cols == 2 * A_COLS + RANK and A_COLS % FOLD_COLS == 0 and QK_W == FOLD_COLS
    assert d % FOLD_ROWS == 0
    w_in_t = jnp.swapaxes(w_in, 1, 2)
    d_out = w_out.shape[2]
    return pl.pallas_call(
        _fold_weights_kernel,
        grid=(depth, d // FOLD_ROWS),
        in_specs=[pl.BlockSpec((None, in_cols, FOLD_ROWS), lambda l, i: (l, 0, i)),
                  pl.BlockSpec((None, RANK, QK_W), lambda l, i: (l, 0, 0)),
                  pl.BlockSpec((None, FOLD_ROWS, d_out), lambda l, i: (l, i, 0))],
        out_specs=[pl.BlockSpec((None, FOLD_ROWS, WIN_COLS), lambda l, i: (l, i, 0)),
                   pl.BlockSpec((None, FOLD_ROWS, d_out), lambda l, i: (l, i, 0))],
        out_shape=[jax.ShapeDtypeStruct((depth, d, WIN_COLS), BF16),
                   jax.ShapeDtypeStruct(w_out.shape, BF16)],
        compiler_params=pltpu.CompilerParams(dimension_semantics=("arbitrary", "arbitrary")),
        name="fold_weights",
    )(w_in_t, gla_w2, w_out)


def _stacked_params(fused_in, wout_bf, norm_w, gla_b, gla_norm_w, ret_norm_w, ret_norm_b):
    row = lambda p: p[:, None, :]
    return (row(norm_w), fused_in, row(gla_b), row(gla_norm_w), row(ret_norm_w), row(ret_norm_b),
            wout_bf)


def kernel(x_prompt, x_sample, state_gla, state_ret, norm_w, w_in, gla_w2, gla_b, gla_norm_w,
           ret_norm_w, ret_norm_b, w_out, final_norm_w):
    bp, tp, d = x_prompt.shape
    bs, ts, _ = x_sample.shape
    assert ts == 1, "the decode path handles one new token per sequence"
    depth = w_in.shape[0]
    tile = min(PROMPT_TILE, tp)
    prompt_tables = _rope_tables(np.arange(tp)) + _retention_tables(tile)
    sample_tables = _rope_tables(PAST_LEN + np.arange(ts))
    fnw = final_norm_w[None, :]

    hp = x_prompt
    hs = x_sample.reshape(bs, d)
    gla_p, ret_p = [], []
    sample_states = None
    fused_in, wout_bf = _fold_weights(w_in, gla_w2, w_out)
    wts = _stacked_params(fused_in, wout_bf, norm_w, gla_b, gla_norm_w, ret_norm_w, ret_norm_b)
    for l in range(depth):
        hp, sg, sr, hs, sample_states = _layer(
            hp, hs, wts, prompt_tables, sample_tables, fnw, state_gla, state_ret, sample_states,
            layer=l, final_norm=l == depth - 1)
        gla_p.append(sg)
        ret_p.append(sr)
    return (hp, hs.reshape(bs, ts, d), jnp.stack(gla_p), jnp.stack(ret_p),
            sample_states[0], sample_states[1])
```

```python
import functools

import numpy as np
import jax
import jax.numpy as jnp
from jax import lax
from jax.experimental import pallas as pl
from jax.experimental.pallas import tpu as pltpu

F32 = jnp.float32
BF16 = jnp.bfloat16

HEADS = 4
DK = 64
DV = 128
QK_W = HEADS * DK
V_W = HEADS * DV
RANK = 16
GATE_TAU = 16.0
CHUNK = 64
ROPE_BASE = 10000.0
PAST_LEN = 16384
EPS = 1e-6
GN_EPS = 1e-5
A_COLS = 2 * QK_W + 2 * V_W
LANES = 128
WIN_COLS = 2 * A_COLS + QK_W
FOLD_COLS = 256
FOLD_ROWS = 256
SUBLANES = 8
VMEM_LIMIT_BYTES = 58 * 1024 * 1024

PROMPT_TILE = 1024
PROJ_ROWS = 256


def _dot(a, b):
    return jnp.dot(a, b, preferred_element_type=F32)


def _dot_tn(a, b):
    return lax.dot_general(a, b, (((0,), (0,)), ((), ())), preferred_element_type=F32)


def _rmsnorm(x, w):
    return x * lax.rsqrt(jnp.mean(x * x, axis=-1, keepdims=True) + EPS) * w


def _silu(x):
    return x * (1.0 / (1.0 + jnp.exp(-x)))


def _log_sigmoid(x):
    return jnp.minimum(x, 0.0) - jnp.log1p(jnp.exp(-jnp.abs(x)))


def _ret_gamma(h):
    return 1.0 - 2.0 ** (-5.0 - h)


def _rope(x, cos, sin_signed):
    lane = lax.broadcasted_iota(jnp.int32, (x.shape[0], LANES), 1)
    first_half = (lane % DK) < (DK // 2)
    outs = []
    for j in range(QK_W // LANES):
        xs = x[:, j * LANES:(j + 1) * LANES]
        swapped = jnp.where(first_half,
                            pltpu.roll(xs, LANES - DK // 2, axis=1),
                            pltpu.roll(xs, DK // 2, axis=1))
        outs.append(xs * cos + swapped * sin_signed)
    return jnp.concatenate(outs, axis=1)


def _split_hi_lo(x):
    hi = x.astype(BF16)
    lo = (x - hi.astype(F32)).astype(BF16)
    return hi, lo


def _col_broadcast(row):
    return jnp.broadcast_to(row, (LANES, row.shape[1])).T


def _dup_transpose(k):
    parts = []
    for c in range(k.shape[0] // CHUNK):
        blk = k[c * CHUNK:(c + 1) * CHUNK]
        parts += [blk, blk]
    return jnp.concatenate(parts, axis=0).T.astype(BF16)


def _block_diag_keys_t(kt2, same_head):
    kt4 = jnp.concatenate([kt2, kt2], axis=1)
    return jnp.where(same_head, kt4, jnp.zeros_like(kt4))


def _pair_dot(lhs, blocks):
    z = jnp.zeros_like(blocks[0])
    outs = []
    for p in range(0, HEADS, 2):
        rhs = jnp.concatenate([jnp.concatenate([blocks[p], z], axis=1),
                               jnp.concatenate([z, blocks[p + 1]], axis=1)], axis=0)
        outs.append(_dot(lhs[:, p * DK:(p + 2) * DK], rhs))
    return jnp.concatenate(outs, axis=1)


def _head_blocks(v):
    return [v[:, h * DV:(h + 1) * DV] for h in range(HEADS)]


def _head_norm_gate(rows, oa_ref, ob_ref, ga_ref, gb_ref, gnw_ref, rnw_ref, rnb_ref):
    parts = []
    for j in range(HEADS):
        sl = slice(j * DV, (j + 1) * DV)
        o = oa_ref[rows, sl]
        y = o * lax.rsqrt(jnp.mean(o * o, axis=-1, keepdims=True) + EPS) * gnw_ref[...]
        parts.append((y * ga_ref[rows, sl]).astype(BF16))
    for j in range(HEADS):
        sl = slice(j * DV, (j + 1) * DV)
        o = ob_ref[rows, sl]
        d = o - jnp.mean(o, axis=-1, keepdims=True)
        var = jnp.mean(d * d, axis=-1, keepdims=True)
        y = (d * lax.rsqrt(var + GN_EPS)) * rnw_ref[:, sl] + rnb_ref[:, sl]
        parts.append((y * gb_ref[rows, sl]).astype(BF16))
    return jnp.concatenate(parts, axis=1)


def _project(x, nw_ref, win_ref, gb_ref):
    pres, zs = [], []
    for r in range(0, x.shape[0], PROJ_ROWS):
        hb = _rmsnorm(x[r:r + PROJ_ROWS], nw_ref[...]).astype(BF16)
        pres.append(_dot(hb, win_ref[:, 0:QK_W]))
        zs.append(_dot(hb, win_ref[:, QK_W:]))
    pre = jnp.concatenate(pres, axis=0)
    z = jnp.concatenate(zs, axis=0)
    la = _log_sigmoid(pre + gb_ref[...]) * (1.0 / GATE_TAU)
    g0, r0 = 0, A_COLS
    gq = z[:, g0:g0 + QK_W] * (DK ** -0.5)
    gk = z[:, g0 + QK_W:g0 + 2 * QK_W]
    gv = z[:, g0 + 2 * QK_W:g0 + 2 * QK_W + V_W]
    gg = z[:, g0 + 2 * QK_W + V_W:r0]
    rq = z[:, r0:r0 + QK_W]
    rk = z[:, r0 + QK_W:r0 + 2 * QK_W]
    rv = z[:, r0 + 2 * QK_W:r0 + 2 * QK_W + V_W]
    rg = z[:, r0 + 2 * QK_W + V_W:2 * A_COLS]
    return gq, gk, gv, gg, la, rq, rk, rv, rg


def _chunk_cumsum(x):
    row_in_chunk = lax.broadcasted_iota(jnp.int32, x.shape, 0) % CHUNK
    shift = 1
    while shift < CHUNK:
        x = x + jnp.where(row_in_chunk >= shift, pltpu.roll(x, shift, axis=0), 0.0)
        shift *= 2
    return x


N_PROMPT_SCRATCH = 15
N_SAMPLE_SCRATCH = 11


def _prompt_rows(r0, x, nw_ref, win_ref, gb_ref, gnw_ref, rnw_ref, rnb_ref, wo_ref,
                 cos, sin, qdec, kdec, dmat, fnw_ref, y_ref,
                 qd_s, kd_s, ke_s, va_s, ga_s, qb_s, qbd_s, kb_s, kbe_s, vb_s, gbt_s,
                 oa_s, ob_s, s_gla, s_ret, *, final_norm, side_work):
    n_rows = x.shape[0]
    sub = slice(r0, r0 + n_rows)
    dup = slice(2 * r0, 2 * (r0 + n_rows))
    gq, gk, gv, gg, la, rq, rk, rv, rg = _project(x, nw_ref, win_ref, gb_ref)
    side_work()

    b = _chunk_cumsum(la)
    chunk_decay_rows = []
    b_last = []
    for c in range(n_rows // CHUNK):
        last = b[(c + 1) * CHUNK - 1:(c + 1) * CHUNK]
        chunk_decay_rows.append(jnp.exp(last))
        b_last.append(jnp.broadcast_to(last, (CHUNK, QK_W)))
    bl = jnp.concatenate(b_last, axis=0)
    qd_s[sub, :] = (gq * jnp.exp(b)).astype(BF16)
    kd_s[:, dup] = _dup_transpose(gk * jnp.exp(-b))
    ke_s[sub, :] = (gk * jnp.exp(bl - b)).astype(BF16)
    va_s[sub, :] = gv.astype(BF16)
    vb_s[sub, :] = rv.astype(BF16)

    qb = _rope(rq, cos, sin)
    kb = _rope(rk, cos, sin) * (DK ** -0.5)
    qb_s[sub, :] = qb.astype(BF16)
    qbd_s[sub, :] = (qb * qdec).astype(BF16)
    kb_s[:, dup] = _dup_transpose(kb)
    kbe_s[sub, :] = (kb * kdec).astype(BF16)
    ga_s[sub, :] = _silu(gg)
    gbt_s[sub, :] = _silu(rg)

    tril = dmat > 0.0
    same_head = (lax.broadcasted_iota(jnp.int32, (QK_W, HEADS * CHUNK), 0) // DK
                 == lax.broadcasted_iota(jnp.int32, (QK_W, HEADS * CHUNK), 1) // CHUNK)
    n_chunks = n_rows // CHUNK
    chunk_rows = [slice(r0 + c * CHUNK, r0 + (c + 1) * CHUNK) for c in range(n_chunks)]
    att_gla, att_ret = [], []
    for rows in chunk_rows:
        cols = slice(2 * rows.start, 2 * rows.stop)
        scores = _dot(qd_s[rows, :], _block_diag_keys_t(kd_s[:, cols], same_head))
        att_gla.append(jnp.where(tril, scores, 0.0).astype(BF16))
        scores = _dot(qb_s[rows, :], _block_diag_keys_t(kb_s[:, cols], same_head))
        att_ret.append((scores * dmat).astype(BF16))
    kv_gla, kv_ret = [], []
    for rows in chunk_rows:
        ke, v = ke_s[rows, :], _head_blocks(va_s[rows, :])
        kv_gla.append([_dot_tn(ke[:, h * DK:(h + 1) * DK], v[h]) for h in range(HEADS)])
        kbe, vr = kbe_s[rows, :], _head_blocks(vb_s[rows, :])
        kv_ret.append([_dot_tn(kbe[:, h * DK:(h + 1) * DK], vr[h]) for h in range(HEADS)])
    sin_gla, sin_ret = [], []
    for c in range(n_chunks):
        sin_gla.append([s.astype(BF16) for s in s_gla])
        sin_ret.append([s.astype(BF16) for s in s_ret])
        ecol = _col_broadcast(chunk_decay_rows[c])
        s_gla = [ecol[h * DK:(h + 1) * DK] * s_gla[h] + kv_gla[c][h] for h in range(HEADS)]
        s_ret = [_ret_gamma(h) ** CHUNK * s_ret[h] + kv_ret[c][h] for h in range(HEADS)]
    for c, rows in enumerate(chunk_rows):
        oa_s[rows, :] = (_pair_dot(att_gla[c], _head_blocks(va_s[rows, :]))
                         + _pair_dot(qd_s[rows, :], sin_gla[c]))
        ob_s[rows, :] = (_pair_dot(att_ret[c], _head_blocks(vb_s[rows, :]))
                         + _pair_dot(qbd_s[rows, :], sin_ret[c]))

    mixed = _head_norm_gate(sub, oa_s, ob_s, ga_s, gbt_s, gnw_ref, rnw_ref, rnb_ref)
    y = x + _dot(mixed, wo_ref[...])
    if final_norm:
        y = _rmsnorm(y, fnw_ref[...])
    y_ref[0, sub, :] = y
    return s_gla, s_ret


def _prompt_tile(x_ref, nw_ref, win_ref, gb_ref, gnw_ref, rnw_ref, rnb_ref, wo_ref,
                 cos_ref, sin_ref, qdec_ref, kdec_ref, dmat_ref, fnw_ref, y_ref,
                 qd_s, kd_s, ke_s, va_s, ga_s, qb_s, qbd_s, kb_s, kbe_s, vb_s, gbt_s,
                 oa_s, ob_s, stg_s, str_s, *, tile, final_norm, side_work):
    dmat = dmat_ref[...]
    s_gla = [stg_s[h] for h in range(HEADS)]
    s_ret = [str_s[h] for h in range(HEADS)]
    s_gla, s_ret = _prompt_rows(
        0, x_ref[0], nw_ref, win_ref, gb_ref, gnw_ref, rnw_ref, rnb_ref, wo_ref,
        cos_ref[...], sin_ref[...], qdec_ref[...], kdec_ref[...], dmat, fnw_ref,
        y_ref, qd_s, kd_s, ke_s, va_s, ga_s, qb_s, qbd_s, kb_s, kbe_s, vb_s, gbt_s,
        oa_s, ob_s, s_gla, s_ret, final_norm=final_norm, side_work=side_work)
    for h in range(HEADS):
        stg_s[h] = s_gla[h]
        str_s[h] = s_ret[h]


def _sample_project(x_ref, nw_ref, win_ref, gb_ref, cos_ref, sin_ref,
                    qa_s, ka_s, al_s, va_s, ga_s, qb_s, kb_s, vb_s, gbt_s):
    gq, gk, gv, gg, la, rq, rk, rv, rg = _project(x_ref[...], nw_ref, win_ref, gb_ref)
    qa_s[...] = gq
    ka_s[...] = gk
    al_s[...] = jnp.exp(la)
    va_s[...] = gv
    ga_s[...] = _silu(gg)
    qb_s[...] = _rope(rq, cos_ref[...], sin_ref[...])
    kb_s[...] = _rope(rk, cos_ref[...], sin_ref[...]) * (DK ** -0.5)
    vb_s[...] = rv
    gbt_s[...] = _silu(rg)


def _sample_update(step, block, sg_in, sr_in, sg_out, sr_out,
                   qa_s, ka_s, al_s, va_s, qb_s, kb_s, vb_s, oa_s, ob_s):
    rows = pl.ds(pl.multiple_of(step * block, block), block)
    qa, ka, al, va = qa_s[rows, :], ka_s[rows, :], al_s[rows, :], va_s[rows, :]
    qb, kb, vb = qb_s[rows, :], kb_s[rows, :], vb_s[rows, :]
    oa_rows, ob_rows = [], []
    for j in range(block):
        one = slice(j, j + 1)
        qcol = _col_broadcast(qa[one])
        kcol = _col_broadcast(ka[one])
        acol = _col_broadcast(al[one])
        qrc = _col_broadcast(qb[one])
        krc = _col_broadcast(kb[one])
        oa_h, ob_h = [], []
        for h in range(HEADS):
            ks = slice(h * DK, (h + 1) * DK)
            vs = slice(h * DV, (h + 1) * DV)
            s_new = acol[ks] * sg_in[j, h] + kcol[ks] * va[one, vs]
            sg_out[j, h] = s_new
            oa_h.append(jnp.sum(qcol[ks] * s_new, axis=0, keepdims=True))
            r_new = _ret_gamma(h) * sr_in[j, h] + krc[ks] * vb[one, vs]
            sr_out[j, h] = r_new
            ob_h.append(jnp.sum(qrc[ks] * r_new, axis=0, keepdims=True))
        oa_rows.append(jnp.concatenate(oa_h, axis=1))
        ob_rows.append(jnp.concatenate(ob_h, axis=1))
    oa_s[rows, :] = jnp.concatenate(oa_rows, axis=0)
    ob_s[rows, :] = jnp.concatenate(ob_rows, axis=0)


def _layer_kernel(*refs, tile, block, layer, final_norm, chained):
    (x_ref, nw_ref, win_ref, gb_ref, gnw_ref, rnw_ref, rnb_ref, wo_ref,
     cos_ref, sin_ref, qdec_ref, kdec_ref, dmat_ref, fnw_ref,
     xs_ref, cos_s_ref, sin_s_ref, sg_in, sr_in) = refs[:19]
    nw_ref, gb_ref, gnw_ref, rnw_ref, rnb_ref = (
        r.at[layer:layer + 1] for r in (nw_ref, gb_ref, gnw_ref, rnw_ref, rnb_ref))
    refs = refs[19 + (4 if chained else 0):]
    y_ref, sg_ref, sr_ref, ys_ref, sg_out, sr_out = refs[:6]
    prompt_scratch = refs[6:6 + N_PROMPT_SCRATCH]
    (qa_s, ka_s, al_s, vsa_s, gsa_s, qsb_s, ksb_s, vsb_s, gsb_s, osa_s, osb_s) = \
        refs[6 + N_PROMPT_SCRATCH:]
    stg_s, str_s = prompt_scratch[-2:]
    t = pl.program_id(1)
    step = pl.program_id(0) * pl.num_programs(1) + t
    last_step = pl.num_programs(0) * pl.num_programs(1) - 1

    @pl.when(t == 0)
    def _zero_prompt_states():
        stg_s[...] = jnp.zeros_like(stg_s)
        str_s[...] = jnp.zeros_like(str_s)

    @pl.when(step == 0)
    def _project_decode_rows():
        _sample_project(xs_ref, nw_ref, win_ref, gb_ref, cos_s_ref, sin_s_ref,
                        qa_s, ka_s, al_s, vsa_s, gsa_s, qsb_s, ksb_s, vsb_s, gsb_s)

    decode_rows = functools.partial(
        _sample_update, step, block, sg_in, sr_in, sg_out, sr_out,
        qa_s, ka_s, al_s, vsa_s, qsb_s, ksb_s, vsb_s, osa_s, osb_s)
    _prompt_tile(x_ref, nw_ref, win_ref, gb_ref, gnw_ref, rnw_ref, rnb_ref, wo_ref,
                 cos_ref, sin_ref, qdec_ref, kdec_ref, dmat_ref, fnw_ref, y_ref,
                 *prompt_scratch, tile=tile, final_norm=final_norm, side_work=decode_rows)

    @pl.when(t == pl.num_programs(1) - 1)
    def _emit_prompt_states():
        sg_ref[0] = stg_s[...]
        sr_ref[0] = str_s[...]

    @pl.when(step == last_step)
    def _finish_decode_rows():
        mixed = _head_norm_gate(slice(None), osa_s, osb_s, gsa_s, gsb_s, gnw_ref, rnw_ref,
                                rnb_ref)
        ys = xs_ref[...] + _dot(mixed, wo_ref[...])
        if final_norm:
            ys = _rmsnorm(ys, fnw_ref[...])
        ys_ref[...] = ys


def _const_spec(shape):
    nd = len(shape)
    return pl.BlockSpec(shape, lambda *_: (0,) * nd)


def _layer_spec(shape, layer):
    nd = len(shape)
    return pl.BlockSpec((None,) + tuple(shape[1:]), lambda *_: (layer,) + (0,) * (nd - 1))


def _layer(x, xs, wts, tables, sample_tables, fnw, sg, sr, prev_out, *, layer, final_norm):
    bsz, seq, d = x.shape
    n = xs.shape[0]
    tile = min(PROMPT_TILE, seq)
    assert seq % tile == 0
    nt = seq // tile
    steps = bsz * nt
    block = n // steps
    assert block * steps == n and block % SUBLANES == 0, "decode rows must split evenly over steps"
    cos, sin, qdec, kdec, dmat = tables
    cos_s, sin_s = sample_tables
    tile_spec = pl.BlockSpec((1, tile, d), lambda b, t: (b, t, 0))
    in_specs = [tile_spec] + [_layer_spec(a.shape, layer) if a.ndim == 3 else _const_spec(a.shape)
                              for a in wts]
    in_specs += [pl.BlockSpec((tile, LANES), lambda b, t: (t, 0)),
                 pl.BlockSpec((tile, LANES), lambda b, t: (t, 0)),
                 _const_spec(qdec.shape), _const_spec(kdec.shape), _const_spec(dmat.shape),
                 _const_spec(fnw.shape), _const_spec(xs.shape), _const_spec(cos_s.shape),
                 _const_spec(sin_s.shape)]
    st_spec = pl.BlockSpec((None, block, HEADS, DK, DV), lambda b, t: (layer, b * nt + t, 0, 0, 0))
    in_specs += [st_spec, st_spec]
    operands = [x, *wts, cos, sin, qdec, kdec, dmat, fnw, xs, cos_s, sin_s, sg, sr]
    aliases = {}
    if prev_out is not None:
        in_specs += [pl.BlockSpec(memory_space=pl.ANY)] * 4
        n_in = len(operands)
        aliases = {n_in: 1, n_in + 1: 2, n_in + 2: 4, n_in + 3: 5}
        operands += list(prev_out)
    depth = sg.shape[0]
    pstate_shape = jax.ShapeDtypeStruct((depth, bsz, HEADS, DK, DV), F32)
    pstate_spec = pl.BlockSpec((None, 1, HEADS, DK, DV), lambda b, t: (layer, b, 0, 0, 0))
    prompt_scratch = (
        [pltpu.VMEM((tile, QK_W), BF16)]
        + [pltpu.VMEM((QK_W, 2 * tile), BF16)]
        + [pltpu.VMEM((tile, QK_W), BF16)]
        + [pltpu.VMEM((tile, V_W), BF16)]
        + [pltpu.VMEM((tile, V_W), F32)]
        + [pltpu.VMEM((tile, QK_W), BF16)] * 2
        + [pltpu.VMEM((QK_W, 2 * tile), BF16)]
        + [pltpu.VMEM((tile, QK_W), BF16)]
        + [pltpu.VMEM((tile, V_W), BF16)]
        + [pltpu.VMEM((tile, V_W), F32)] * 3
        + [pltpu.VMEM((HEADS, DK, DV), F32)] * 2
    )
    sample_scratch = (
        [pltpu.VMEM((n, QK_W), F32)] * 3
        + [pltpu.VMEM((n, V_W), F32)] * 2
        + [pltpu.VMEM((n, QK_W), F32)] * 2
        + [pltpu.VMEM((n, V_W), F32)] * 4
    )
    assert len(prompt_scratch) == N_PROMPT_SCRATCH and len(sample_scratch) == N_SAMPLE_SCRATCH
    y, sgp, srp, ys, sg_new, sr_new = pl.pallas_call(
        functools.partial(_layer_kernel, tile=tile, block=block, layer=layer,
                          final_norm=final_norm, chained=prev_out is not None),
        grid=(bsz, nt),
        in_specs=in_specs,
        out_specs=[tile_spec, pstate_spec, pstate_spec, _const_spec(xs.shape), st_spec, st_spec],
        out_shape=[jax.ShapeDtypeStruct(x.shape, F32), pstate_shape, pstate_shape,
                   jax.ShapeDtypeStruct(xs.shape, F32),
                   jax.ShapeDtypeStruct(sg.shape, F32), jax.ShapeDtypeStruct(sr.shape, F32)],
        input_output_aliases=aliases,
        scratch_shapes=prompt_scratch + sample_scratch,
        compiler_params=pltpu.CompilerParams(
            dimension_semantics=("arbitrary", "arbitrary"),
            vmem_limit_bytes=VMEM_LIMIT_BYTES),
        name="hybrid_layer",
    )(*operands)
    return y, ys, (sgp, srp, sg_new, sr_new)


def _rope_tables(pos):
    half = DK // 2
    inv = ROPE_BASE ** (-np.arange(half, dtype=np.float64) / half)
    ang = np.asarray(pos, dtype=np.float64)[:, None] * inv[None, :]
    cos = np.tile(np.cos(ang), (1, LANES // half))
    sin = np.tile(np.concatenate([-np.sin(ang), np.sin(ang)], axis=1), (1, LANES // DK))
    return jnp.asarray(cos, F32), jnp.asarray(sin, F32)


def _retention_tables(tile):
    idx = np.arange(CHUNK, dtype=np.float64)
    gam = np.array([_ret_gamma(h) for h in range(HEADS)], dtype=np.float64)
    qdec = np.repeat(gam[None, :] ** (idx[:, None] + 1.0), DK, axis=1)
    kdec = np.repeat(gam[None, :] ** (CHUNK - 1.0 - idx[:, None]), DK, axis=1)
    rel = idx[:, None] - idx[None, :]
    dm = [np.where(rel >= 0, gam[h] ** np.maximum(rel, 0.0), 0.0) for h in range(HEADS)]
    dmat = np.concatenate(dm, axis=1)
    reps = tile // CHUNK
    return (jnp.asarray(np.tile(qdec, (reps, 1)), F32), jnp.asarray(np.tile(kdec, (reps, 1)), F32),
            jnp.asarray(dmat, F32))


def _fold_weights_kernel(wt_ref, w2_ref, wout_ref, fused_ref, wout_bf_ref):
    a_hi, a_lo = _split_hi_lo(wt_ref[A_COLS:A_COLS + RANK, :])
    b_hi, b_lo = _split_hi_lo(w2_ref[...])
    folded = _dot_tn(a_hi, b_hi) + _dot_tn(a_hi, b_lo) + _dot_tn(a_lo, b_hi)
    fused_ref[:, 0:QK_W] = folded.astype(BF16)
    for j in range(1, WIN_COLS // FOLD_COLS):
        start = (j - 1) * FOLD_COLS + (RANK if j > A_COLS // FOLD_COLS else 0)
        fused_ref[:, j * FOLD_COLS:(j + 1) * FOLD_COLS] = \
            wt_ref[start:start + FOLD_COLS, :].astype(BF16).T
    wout_bf_ref[...] = wout_ref[...].astype(BF16)


def _fold_weights(w_in, gla_w2, w_out):
    depth, d, in_cols = w_in.shape
    assert in_cols == 2 * A_COLS + RANK and A_COLS % FOLD_COLS == 0 and QK_W == FOLD_COLS
    assert d % FOLD_ROWS == 0
    w_in_t = jnp.swapaxes(w_in, 1, 2)
    d_out = w_out.shape[2]
    return pl.pallas_call(
        _fold_weights_kernel,
        grid=(depth, d // FOLD_ROWS),
        in_specs=[pl.BlockSpec((None, in_cols, FOLD_ROWS), lambda l, i: (l, 0, i)),
                  pl.BlockSpec((None, RANK, QK_W), lambda l, i: (l, 0, 0)),
                  pl.BlockSpec((None, FOLD_ROWS, d_out), lambda l, i: (l, i, 0))],
        out_specs=[pl.BlockSpec((None, FOLD_ROWS, WIN_COLS), lambda l, i: (l, i, 0)),
                   pl.BlockSpec((None, FOLD_ROWS, d_out), lambda l, i: (l, i, 0))],
        out_shape=[jax.ShapeDtypeStruct((depth, d, WIN_COLS), BF16),
                   jax.ShapeDtypeStruct(w_out.shape, BF16)],
        compiler_params=pltpu.CompilerParams(dimension_semantics=("arbitrary", "arbitrary")),
        name="fold_weights",
    )(w_in_t, gla_w2, w_out)


def _stacked_params(fused_in, wout_bf, norm_w, gla_b, gla_norm_w, ret_norm_w, ret_norm_b):
    return (norm_w, fused_in, gla_b, gla_norm_w, ret_norm_w, ret_norm_b, wout_bf)


def kernel(x_prompt, x_sample, state_gla, state_ret, norm_w, w_in, gla_w2, gla_b, gla_norm_w,
           ret_norm_w, ret_norm_b, w_out, final_norm_w):
    bp, tp, d = x_prompt.shape
    bs, ts, _ = x_sample.shape
    assert ts == 1, "the decode path handles one new token per sequence"
    depth = w_in.shape[0]
    tile = min(PROMPT_TILE, tp)
    prompt_tables = _rope_tables(np.arange(tp)) + _retention_tables(tile)
    sample_tables = _rope_tables(PAST_LEN + np.arange(ts))
    fnw = final_norm_w[None, :]

    hp = x_prompt
    hs = x_sample.reshape(bs, d)
    states = None
    fused_in, wout_bf = _fold_weights(w_in, gla_w2, w_out)
    wts = _stacked_params(fused_in, wout_bf, norm_w, gla_b, gla_norm_w, ret_norm_w, ret_norm_b)
    for l in range(depth):
        hp, hs, states = _layer(
            hp, hs, wts, prompt_tables, sample_tables, fnw, state_gla, state_ret, states,
            layer=l, final_norm=l == depth - 1)
    return (hp, hs.reshape(bs, ts, d)) + tuple(states)
```
